```python
import jax
import jax.numpy as jnp
from jax import lax
import numpy as np

D_MODEL = 1024
BATCH = 4
SEQ = 4096
DEPTH = 2

GRID_W = 64
CTX_LEN = 256
EPS = 1e-6
N_MOD = 9
D_FF = 2816

CONV_A_W = 256
CONV_A_K = 3
SGU_W = 256
SGU_GROUPS = 4
SGU_GROUP_W = SGU_W // SGU_GROUPS
CHUNK = 128
DA_HEADS = 4
DA_HEAD_DIM = 64
DA_W = DA_HEADS * 2 * DA_HEAD_DIM
ROPE_BASE = 10000.0
ROPE_NF = DA_HEAD_DIM // 4
Q_BLOCK = 128
CONF_W = 256
CONF_K = 31
N_BRANCH = 4

A_OFF = 0
B_OFF = A_OFF + 3 * CONV_A_W
Q_OFF = B_OFF + 2 * SGU_W
K_OFF = Q_OFF + DA_W
V_OFF = K_OFF + DA_W
D_OFF = V_OFF + DA_W
G_OFF = D_OFF + 2 * CONF_W
IN_COLS = G_OFF + N_BRANCH * D_MODEL

kernel_name = "hybrid_gated_mixers_dit_block"


def rmsnorm(x, g):
    xf = x.astype(jnp.float32)
    y = xf * lax.rsqrt(jnp.mean(xf * xf, axis=-1, keepdims=True) + EPS)
    return (y * g.astype(jnp.float32)).astype(x.dtype)


def layernorm_plain(x):
    xf = x.astype(jnp.float32)
    mu = jnp.mean(xf, axis=-1, keepdims=True)
    var = jnp.mean(jnp.square(xf - mu), axis=-1, keepdims=True)
    return ((xf - mu) * lax.rsqrt(var + EPS)).astype(x.dtype)


def layernorm_affine(x, g, b):
    return (layernorm_plain(x) * g + b).astype(x.dtype)


def modulated_norm(x, g, shift, scale):
    return rmsnorm(x, g) * (1 + scale) + shift


def swiglu(h, w1, w3, w2):
    return (jax.nn.silu(h @ w1) * (h @ w3)) @ w2


def dwconv_centred(x, w):
    k = w.shape[0]
    return lax.conv_general_dilated(
        x, w[:, None, :].astype(x.dtype), window_strides=(1,),
        padding=[(k // 2, k // 2)], dimension_numbers=("NWC", "WIO", "NWC"),
        feature_group_count=x.shape[-1])


def axial_rope_tables(n_tokens):
    n_rows = n_tokens // GRID_W
    row = jnp.repeat(jnp.arange(n_rows, dtype=jnp.float32), GRID_W)
    col = jnp.tile(jnp.arange(GRID_W, dtype=jnp.float32), n_rows)
    inv = ROPE_BASE ** (-jnp.arange(ROPE_NF, dtype=jnp.float32) / ROPE_NF)
    ang = jnp.stack([row[:, None] * inv, col[:, None] * inv], axis=1)
    ang = ang[:, None, None]
    return jnp.cos(ang), jnp.sin(ang)


def apply_rope(x, cos, sin):
    xs = x.reshape(x.shape[:-1] + (2, 2, ROPE_NF))
    x1, x2 = xs[..., 0, :], xs[..., 1, :]
    cos, sin = cos.astype(x.dtype), sin.astype(x.dtype)
    out = jnp.stack([x1 * cos - x2 * sin, x2 * cos + x1 * sin], axis=-2)
    return out.reshape(x.shape)


def heads_qk(p):
    return p.reshape(p.shape[:2] + (DA_HEADS, 2, DA_HEAD_DIM))


def heads_v(p):
    return p.reshape(p.shape[:2] + (DA_HEADS, 2 * DA_HEAD_DIM))


def diff_attend(q, k, v, lam):
    s = jnp.einsum("bqhmd,bkhmd->bmhqk", q, k).astype(jnp.float32) * (DA_HEAD_DIM ** -0.5)
    p = jax.nn.softmax(s, axis=-1)
    a = p[:, 0] - lam * p[:, 1]
    return jnp.einsum("bhqk,bkhe->bqhe", a.astype(v.dtype), v)


def diff_attn_out(o, subln_g, lam_init, w_c_out):
    o = rmsnorm(o, subln_g) * (1 - lam_init)
    return o.reshape(o.shape[:2] + (DA_W,)) @ w_c_out


def short_conv_branch(p, conv_w, w_out):
    bg, cg, xin = jnp.split(p[..., A_OFF:B_OFF], 3, axis=-1)
    return (bg * dwconv_centred(cg * xin, conv_w)) @ w_out


def sgu_branch(p, w_s, b_s, w_out):
    z = jax.nn.gelu(p[..., B_OFF:Q_OFF])
    u, v = jnp.split(z, 2, axis=-1)
    v = layernorm_plain(v)
    bsz, t = v.shape[:2]
    vc = v.reshape(bsz, t // CHUNK, CHUNK, SGU_GROUPS, SGU_GROUP_W)
    s = jnp.einsum("gpq,bnqgc->bnpgc", w_s, vc) + b_s.T[:, :, None]
    return (u * s.reshape(bsz, t, SGU_W)) @ w_out


def conformer_conv_branch(p, dw, db, ln_g, ln_b, w_out):
    z = p[..., D_OFF:G_OFF]
    h = z[..., :CONF_W] * jax.nn.sigmoid(z[..., CONF_W:])
    h = dwconv_centred(h, dw) + db
    h = jax.nn.silu(layernorm_affine(h, ln_g, ln_b))
    return h @ w_out


def mix_stream(p, yc, conv_a_w, w_a_out, sgu_w, sgu_b, w_b_out,
               conf_dw, conf_db, conf_ln_g, conf_ln_b, w_d_out, w_o):
    ya = short_conv_branch(p, conv_a_w, w_a_out)
    yb = sgu_branch(p, sgu_w, sgu_b, w_b_out)
    yd = conformer_conv_branch(p, conf_dw, conf_db, conf_ln_g, conf_ln_b, w_d_out)
    g = jax.nn.sigmoid(p[..., G_OFF:]).reshape(p.shape[:2] + (N_BRANCH, D_MODEL))
    merged = g[:, :, 0] * ya + g[:, :, 1] * yb + g[:, :, 2] * yc + g[:, :, 3] * yd
    return merged @ w_o


def setup_inputs(seed: int = 0) -> dict:
    key = jax.random.key(seed)
    ks = iter(jax.random.split(key, 32))

    def nrm(shape, scale):
        return scale * jax.random.normal(next(ks), shape, jnp.float32)

    L, D = DEPTH, D_MODEL
    return {
        "x": nrm((BATCH, SEQ, D), 1.0),
        "c": nrm((BATCH, D), 1.0),
        "ctx": nrm((BATCH, CTX_LEN, D), 1.0),
        "c_ctx": nrm((D,), 1.0),
        "w_ada": nrm((L, D, N_MOD * D), 0.5 * D ** -0.5),
        "b_ada": nrm((L, N_MOD * D), 0.02),
        "norm_g": 1.0 + nrm((L, 3, D), 0.02),
        "ffn1_w1": nrm((L, D, D_FF), D ** -0.5),
        "ffn1_w3": nrm((L, D, D_FF), D ** -0.5),
        "ffn1_w2": nrm((L, D_FF, D), D_FF ** -0.5),
        "ffn2_w1": nrm((L, D, D_FF), D ** -0.5),
        "ffn2_w3": nrm((L, D, D_FF), D ** -0.5),
        "ffn2_w2": nrm((L, D_FF, D), D_FF ** -0.5),
        "w_in": nrm((L, D, IN_COLS), D ** -0.5),
        "conv_a_w": nrm((L, CONV_A_K, CONV_A_W), CONV_A_K ** -0.5),
        "w_a_out": nrm((L, CONV_A_W, D), CONV_A_W ** -0.5),
        "sgu_w": nrm((L, SGU_GROUPS, CHUNK, CHUNK), CHUNK ** -0.5),
        "sgu_b": nrm((L, SGU_GROUPS, CHUNK), 0.02),
        "w_b_out": nrm((L, SGU_W, D), SGU_W ** -0.5),
        "lam_p": nrm((L, 4, DA_HEAD_DIM), 0.1),
        "subln_g": 1.0 + nrm((L, 2 * DA_HEAD_DIM), 0.02),
        "w_c_out": nrm((L, DA_W, D), DA_W ** -0.5),
        "conf_dw": nrm((L, CONF_K, CONF_W), CONF_K ** -0.5),
        "conf_db": nrm((L, CONF_W), 0.02),
        "conf_ln_g": 1.0 + nrm((L, CONF_W), 0.02),
        "conf_ln_b": nrm((L, CONF_W), 0.02),
        "w_d_out": nrm((L, CONF_W, D), CONF_W ** -0.5),
        "w_o": nrm((L, D, D), D ** -0.5),
        "final_g": 1.0 + nrm((D,), 0.02),
    }


def reference(x, c, ctx, c_ctx, w_ada, b_ada, norm_g, ffn1_w1, ffn1_w3, ffn1_w2,
              ffn2_w1, ffn2_w3, ffn2_w2, w_in, conv_a_w, w_a_out, sgu_w, sgu_b,
              w_b_out, lam_p, subln_g, w_c_out, conf_dw, conf_db, conf_ln_g,
              conf_ln_b, w_d_out, w_o, final_g):
    bsz, n_tok = x.shape[:2]
    n_blk = n_tok // Q_BLOCK
    cos, sin = axial_rope_tables(n_tok)
    sc = jax.nn.silu(c)[:, None, :]
    scc = jax.nn.silu(c_ctx)[None, None, :]
    for l in range(DEPTH):
        last = l == DEPTH - 1
        mx = jnp.split(sc @ w_ada[l] + b_ada[l], N_MOD, axis=-1)
        mc = jnp.split(scc @ w_ada[l] + b_ada[l], N_MOD, axis=-1)
        lam_init = 0.8 - 0.6 * float(np.exp(-0.3 * l))
        lp = lam_p[l].astype(jnp.float32)
        lam = jnp.exp(jnp.sum(lp[0] * lp[1])) - jnp.exp(jnp.sum(lp[2] * lp[3])) + lam_init
        ffn1 = (ffn1_w1[l], ffn1_w3[l], ffn1_w2[l])
        ffn2 = (ffn2_w1[l], ffn2_w3[l], ffn2_w2[l])
        lw = (conv_a_w[l], w_a_out[l], sgu_w[l], sgu_b[l], w_b_out[l],
              conf_dw[l], conf_db[l], conf_ln_g[l], conf_ln_b[l], w_d_out[l], w_o[l])

        x = x + 0.5 * mx[2] * swiglu(modulated_norm(x, norm_g[l, 0], mx[0], mx[1]), *ffn1)
        ctx = ctx + 0.5 * mc[2] * swiglu(modulated_norm(ctx, norm_g[l, 0], mc[0], mc[1]), *ffn1)

        hx = modulated_norm(x, norm_g[l, 1], mx[3], mx[4])
        hc = modulated_norm(ctx, norm_g[l, 1], mc[3], mc[4])
        px = hx @ w_in[l]
        if last:
            pc_kv = hc @ w_in[l][:, K_OFF:D_OFF]
            kc, vc = heads_qk(pc_kv[..., :DA_W]), heads_v(pc_kv[..., DA_W:])
        else:
            pc = hc @ w_in[l]
            qc = heads_qk(pc[..., Q_OFF:K_OFF])
            kc = heads_qk(pc[..., K_OFF:V_OFF])
            vc = heads_v(pc[..., V_OFF:D_OFF])
        qx = apply_rope(heads_qk(px[..., Q_OFF:K_OFF]), cos, sin)
        kx = apply_rope(heads_qk(px[..., K_OFF:V_OFF]), cos, sin)
        vx = heads_v(px[..., V_OFF:D_OFF])
        k_all = jnp.concatenate([kx, kc], axis=1)
        v_all = jnp.concatenate([vx, vc], axis=1)
        qb = jnp.moveaxis(qx.reshape((bsz, n_blk, Q_BLOCK) + qx.shape[2:]), 1, 0)
        ob = lax.map(lambda qi: diff_attend(qi, k_all, v_all, lam), qb)
        ox = jnp.moveaxis(ob, 0, 1).reshape((bsz, n_tok) + ob.shape[3:])
        yc_x = diff_attn_out(ox, subln_g[l], lam_init, w_c_out[l])
        x = x + mx[5] * mix_stream(px, yc_x, *lw)

        x = x + 0.5 * mx[8] * swiglu(modulated_norm(x, norm_g[l, 2], mx[6], mx[7]), *ffn2)

        if not last:
            oc = diff_attend(qc, kc, vc, lam)
            yc_c = diff_attn_out(oc, subln_g[l], lam_init, w_c_out[l])
            ctx = ctx + mc[5] * mix_stream(pc, yc_c, *lw)
            ctx = ctx + 0.5 * mc[8] * swiglu(modulated_norm(ctx, norm_g[l, 2], mc[6], mc[7]), *ffn2)

    return rmsnorm(x, final_g)
```

```python
import functools
import math

import jax
import jax.numpy as jnp
import numpy as np
from jax import lax
from jax.experimental import pallas as pl
from jax.experimental.pallas import tpu as pltpu

F32 = jnp.float32
BF16 = jnp.bfloat16

EPS = 1e-6
GRID_W = 64
N_MOD = 9
ROPE_BASE = 10000.0

CONV_A_W = 256
CONV_A_K = 3
SGU_W = 256
SGU_GROUPS = 4
SGU_GROUP_W = SGU_W // SGU_GROUPS
CHUNK = 128
DA_HEADS = 4
DA_HEAD_DIM = 64
HEAD_W = 2 * DA_HEAD_DIM
DA_W = DA_HEADS * HEAD_W
ROPE_NF = DA_HEAD_DIM // 4
CONF_W = 256
CONF_K = 31
N_BRANCH = 4

A_OFF = 0
B_OFF = A_OFF + 3 * CONV_A_W
Q_OFF = B_OFF + 2 * SGU_W
K_OFF = Q_OFF + DA_W
V_OFF = K_OFF + DA_W
D_OFF = V_OFF + DA_W
G_OFF = D_OFF + 2 * CONF_W

TM_TOK = 512
TM_MIX = 256
TQ = 512
KC = 256
HALO = 16
MOD_ROWS = 8
VMEM_LIMIT = 56 * 1024 * 1024


def _cparams(n_grid):
    return pltpu.CompilerParams(dimension_semantics=("arbitrary",) * n_grid,
                                vmem_limit_bytes=VMEM_LIMIT)


def _resident(shape):
    nd = len(shape)
    return pl.BlockSpec(shape, lambda *_: (0,) * nd, pipeline_mode=pl.Buffered(1))


def _dot(a, b):
    return jnp.dot(a, b, preferred_element_type=F32)


def _modnorm(x, g, shift, scale):
    y = x * lax.rsqrt(jnp.mean(x * x, axis=-1, keepdims=True) + EPS)
    return (y * g) * (1.0 + scale) + shift


def _ada_kernel(c_ref, w_ref, b_ref, o_ref):
    c = c_ref[...]
    s = (c * jax.nn.sigmoid(c)).astype(BF16)
    o_ref[...] = _dot(s, w_ref[...].astype(BF16)) + b_ref[...]


def _ada_call(cvec, w_ada, b_ada):
    depth, d, n = w_ada.shape
    tn = math.gcd(n, 1536)
    return pl.pallas_call(
        _ada_kernel,
        grid=(depth, n // tn),
        in_specs=[
            pl.BlockSpec((MOD_ROWS, d), lambda l, j: (0, 0)),
            pl.BlockSpec((None, d, tn), lambda l, j: (l, 0, j)),
            pl.BlockSpec((None, 1, tn), lambda l, j: (l, 0, j)),
        ],
        out_specs=pl.BlockSpec((None, MOD_ROWS, tn), lambda l, j: (l, 0, j)),
        out_shape=jax.ShapeDtypeStruct((depth, MOD_ROWS, n), F32),
        compiler_params=_cparams(2),
        name="ada_mod",
    )(cvec, w_ada, b_ada.reshape(depth, 1, n))


def _ffn_chunks(d_ff):
    tiles = d_ff // 256
    assert tiles * 256 == d_ff and tiles >= 2
    cut = (tiles + 1) // 2 * 256
    return ((0, cut), (cut, d_ff))


def _ffn_kernel(*refs, n_x_tiles, two_inputs, final, mod_base, d_ff):
    if two_inputs:
        x_ref, c_ref, *refs = refs
    else:
        x_ref, *refs = refs
    if final:
        mod_ref, g_ref, w1_ref, w3_ref, w2_ref, fg_ref, o_ref = refs
    else:
        mod_ref, g_ref, w1_ref, w3_ref, w2_ref, o_ref = refs
    x = x_ref[...]
    if two_inputs:
        x = jnp.where(pl.program_id(0) < n_x_tiles, x, c_ref[...])
    shift = mod_ref[mod_base:mod_base + 1, :]
    scale = mod_ref[mod_base + 1:mod_base + 2, :]
    gate = mod_ref[mod_base + 2:mod_base + 3, :]
    h = _modnorm(x, g_ref[...], shift, scale).astype(BF16)
    acc = None
    for lo, hi in _ffn_chunks(d_ff):
        a = _dot(h, w1_ref[:, lo:hi])
        b = _dot(h, w3_ref[:, lo:hi])
        gch = (a * jax.nn.sigmoid(a) * b).astype(BF16)
        part = _dot(gch, w2_ref[lo:hi, :])
        acc = part if acc is None else acc + part
    y = x + (0.5 * gate) * acc
    if final:
        y = y * lax.rsqrt(jnp.mean(y * y, axis=-1, keepdims=True) + EPS) * fg_ref[...]
    o_ref[...] = y


def _mod_row(i, n_x_tiles, tiles_per_batch, n_batch):
    return jnp.where(i < n_x_tiles, i // tiles_per_batch, n_batch)


def _ffn_call(xs, ctx2d, mods, layer, mod_base, g, w1, w3, w2, final_g, *, n_batch, seq, n_out_rows):
    d = xs.shape[1]
    d_ff = w1.shape[1]
    tm = TM_TOK
    n_x_tiles = n_batch * seq // tm
    n_tiles = n_out_rows // tm
    tiles_per_batch = seq // tm
    two_inputs = ctx2d is not None
    final = final_g is not None
    in_specs = []
    args = []
    if two_inputs:
        in_specs.append(pl.BlockSpec((tm, d), lambda i: (jnp.minimum(i, n_x_tiles - 1), 0)))
        in_specs.append(pl.BlockSpec((tm, d), lambda i: (jnp.maximum(i - n_x_tiles, 0), 0)))
        args += [xs, ctx2d]
    else:
        in_specs.append(pl.BlockSpec((tm, d), lambda i: (i, 0)))
        args.append(xs)
    in_specs += [
        pl.BlockSpec((None, None, N_MOD, d),
                     lambda i: (layer, _mod_row(i, n_x_tiles, tiles_per_batch, n_batch), 0, 0)),
        _resident((1, d)),
        _resident((d, d_ff)),
        _resident((d, d_ff)),
        _resident((d_ff, d)),
    ]
    args += [mods, g, w1, w3, w2]
    if final:
        in_specs.append(_resident((1, d)))
        args.append(final_g)
    kern = functools.partial(_ffn_kernel, n_x_tiles=n_x_tiles, two_inputs=two_inputs, final=final,
                             mod_base=mod_base, d_ff=d_ff)
    return pl.pallas_call(
        kern,
        grid=(n_tiles,),
        in_specs=in_specs,
        out_specs=pl.BlockSpec((tm, d), lambda i: (i, 0)),
        out_shape=jax.ShapeDtypeStruct((n_out_rows, d), F32),
        compiler_params=_cparams(1),
        name="ffn",
    )(*args)


def _gelu_tanh(x):
    c = math.sqrt(2.0 / math.pi)
    return x * (0.5 * (1.0 + jnp.tanh(c * (x + 0.044715 * (x * x * x)))))


def _proj_kernel(x_ref, mod_ref, g_ref, w_ref, cos_ref, sin_ref,
                 pa_ref, pb_ref, qT_ref, k_ref, vT_ref, hd_ref):
    x = x_ref[...]
    h = _modnorm(x, g_ref[...], mod_ref[3:4, :], mod_ref[4:5, :]).astype(BF16)

    def proj(lo, hi):
        return _dot(h, w_ref[:, lo:hi])

    pa = proj(A_OFF, B_OFF)
    pa_ref[:, 0:CONV_A_W] = pa[:, 0:CONV_A_W].astype(BF16)
    pa_ref[:, CONV_A_W:] = (pa[:, CONV_A_W:2 * CONV_A_W] * pa[:, 2 * CONV_A_W:]).astype(BF16)

    z = _gelu_tanh(proj(B_OFF, Q_OFF))
    v = z[:, SGU_W:]
    vc = v - jnp.mean(v, axis=-1, keepdims=True)
    vn = vc * lax.rsqrt(jnp.mean(vc * vc, axis=-1, keepdims=True) + EPS)
    pb_ref[:, 0:SGU_W] = z[:, 0:SGU_W].astype(BF16)
    pb_ref[:, SGU_W:] = vn.astype(BF16)

    cos = cos_ref[...]
    sin = sin_ref[...]
    lane = lax.broadcasted_iota(jnp.int32, cos.shape, 1)
    first_half = (lane % (2 * ROPE_NF)) < ROPE_NF

    def rope(blk):
        partner = jnp.where(first_half, pltpu.roll(blk, HEAD_W - ROPE_NF, 1), pltpu.roll(blk, ROPE_NF, 1))
        return blk * cos + partner * sin

    q = proj(Q_OFF, K_OFF)
    for hh in range(DA_HEADS):
        sl = slice(hh * HEAD_W, (hh + 1) * HEAD_W)
        qr = rope(q[:, sl]) * (DA_HEAD_DIM ** -0.5)
        qT_ref[sl, :] = qr.T.astype(BF16)
    k = proj(K_OFF, V_OFF)
    for hh in range(DA_HEADS):
        sl = slice(hh * HEAD_W, (hh + 1) * HEAD_W)
        k_ref[:, sl] = rope(k[:, sl]).astype(BF16)
    vv = proj(V_OFF, D_OFF)
    for hh in range(DA_HEADS):
        sl = slice(hh * HEAD_W, (hh + 1) * HEAD_W)
        vT_ref[sl, :] = vv[:, sl].T.astype(BF16)

    zd = proj(D_OFF, G_OFF)
    hd_ref[...] = (zd[:, 0:CONF_W] * jax.nn.sigmoid(zd[:, CONF_W:])).astype(BF16)


def _proj_call(xs, mods, layer, g, w_small, cos_t, sin_t, *, n_batch, seq):
    ntok, d = xs.shape
    tm = TM_TOK
    n_x_tiles = n_batch * seq // tm
    tiles_per_batch = seq // tm
    row = lambda i: (i, 0)
    col = lambda i: (0, i)
    outs = pl.pallas_call(
        _proj_kernel,
        grid=(ntok // tm,),
        in_specs=[
            pl.BlockSpec((tm, d), row),
            pl.BlockSpec((None, None, N_MOD, d),
                         lambda i: (layer, _mod_row(i, n_x_tiles, tiles_per_batch, n_batch), 0, 0)),
            _resident((1, d)),
            _resident((d, G_OFF)),
            pl.BlockSpec((tm, HEAD_W), row),
            pl.BlockSpec((tm, HEAD_W), row),
        ],
        out_specs=[
            pl.BlockSpec((tm, 2 * CONV_A_W), row),
            pl.BlockSpec((tm, 2 * SGU_W), row),
            pl.BlockSpec((DA_W, tm), col),
            pl.BlockSpec((tm, DA_W), row),
            pl.BlockSpec((DA_W, tm), col),
            pl.BlockSpec((tm, CONF_W), row),
        ],
        out_shape=[
            jax.ShapeDtypeStruct((ntok, 2 * CONV_A_W), BF16),
            jax.ShapeDtypeStruct((ntok, 2 * SGU_W), BF16),
            jax.ShapeDtypeStruct((DA_W, ntok), BF16),
            jax.ShapeDtypeStruct((ntok, DA_W), BF16),
            jax.ShapeDtypeStruct((DA_W, ntok), BF16),
            jax.ShapeDtypeStruct((ntok, CONF_W), BF16),
        ],
        compiler_params=_cparams(1),
        name="mixer_proj",
    )(xs, mods, g, w_small, cos_t, sin_t)
    return outs


def _attn_kernel(*refs, n_lat_chunks, lam_init, tq):
    if n_lat_chunks:
        lam_ref, sg_ref, qT_ref, kl_ref, vl_ref, kc_ref, vc_ref, o_ref, s_scr = refs
    else:
        lam_ref, sg_ref, qT_ref, kc_ref, vc_ref, o_ref, s_scr = refs
        kl_ref = vl_ref = None
    n_ctx = kc_ref.shape[0]
    lat_rows = n_lat_chunks * KC

    lp = lam_ref[...]
    lam = (jnp.exp(jnp.sum(lp[0:1, :] * lp[1:2, :], axis=-1, keepdims=True))
           - jnp.exp(jnp.sum(lp[2:3, :] * lp[3:4, :], axis=-1, keepdims=True)) + lam_init)
    sub = lax.broadcasted_iota(jnp.int32, (HEAD_W, tq), 0)

    for hh in range(DA_HEADS):
        hs = slice(hh * HEAD_W, (hh + 1) * HEAD_W)
        qh = qT_ref[hs, :]
        o_maps = []
        for m in range(2):
            in_map = (sub < DA_HEAD_DIM) if m == 0 else (sub >= DA_HEAD_DIM)
            qm = jnp.where(in_map, qh, jnp.zeros_like(qh))

            def score_lat(c, mx):
                r0 = pl.multiple_of(c * KC, KC)
                s = _dot(kl_ref[pl.ds(r0, KC), hs], qm)
                s_scr[pl.ds(r0, KC), :] = s
                return jnp.maximum(mx, jnp.max(s, axis=0, keepdims=True))

            mx = jnp.full((1, tq), -jnp.inf, F32)
            if n_lat_chunks:
                mx = lax.fori_loop(0, n_lat_chunks, score_lat, mx)
            s = _dot(kc_ref[:, hs], qm)
            s_scr[lat_rows:lat_rows + n_ctx, :] = s
            mx = jnp.maximum(mx, jnp.max(s, axis=0, keepdims=True))

            def pv_lat(c, carry):
                lsum, acc = carry
                r0 = pl.multiple_of(c * KC, KC)
                p = jnp.exp(s_scr[pl.ds(r0, KC), :] - mx)
                lsum = lsum + jnp.sum(p, axis=0, keepdims=True)
                acc = acc + _dot(vl_ref[hs, pl.ds(r0, KC)], p.astype(BF16))
                return lsum, acc

            carry = (jnp.zeros((1, tq), F32), jnp.zeros((HEAD_W, tq), F32))
            if n_lat_chunks:
                carry = lax.fori_loop(0, n_lat_chunks, pv_lat, carry)
            lsum, acc = carry
            p = jnp.exp(s_scr[lat_rows:lat_rows + n_ctx, :] - mx)
            lsum = lsum + jnp.sum(p, axis=0, keepdims=True)
            acc = acc + _dot(vc_ref[hs, :], p.astype(BF16))
            o_maps.append(acc / lsum)

        oh = o_maps[0] - lam * o_maps[1]
        y = oh * lax.rsqrt(jnp.mean(oh * oh, axis=0, keepdims=True) + EPS)
        y = y * (sg_ref[...] * (1.0 - lam_init))
        o_ref[:, hs] = y.T.astype(BF16)


def _attn_call(lam_p_l, sg_col, qT, k, vT, lam_init, *, n_batch, seq, ctx_len, latent):
    ntok = k.shape[0]
    n_x = n_batch * seq
    ctx_blk0 = n_x // ctx_len
    if latent:
        tq = TQ
        n_q = seq // tq
        n_lat_chunks = seq // KC
        n_keys = seq + ctx_len
        q_map = lambda b, j: (0, b * n_q + j)
        o_map = lambda b, j: (b * n_q + j, 0)
        out_rows = n_x
    else:
        tq = ctx_len
        n_q = 1
        n_lat_chunks = 0
        n_keys = ctx_len
        q_map = lambda b, j: (0, ctx_blk0 + b)
        o_map = lambda b, j: (b, 0)
        out_rows = n_batch * ctx_len
    in_specs = [
        _resident(lam_p_l.shape),
        _resident(sg_col.shape),
        pl.BlockSpec((DA_W, tq), q_map),
    ]
    args = [lam_p_l, sg_col, qT]
    if latent:
        in_specs += [pl.BlockSpec((seq, DA_W), lambda b, j: (b, 0)),
                     pl.BlockSpec((DA_W, seq), lambda b, j: (0, b))]
        args += [k, vT]
    in_specs += [pl.BlockSpec((ctx_len, DA_W), lambda b, j: (ctx_blk0 + b, 0)),
                 pl.BlockSpec((DA_W, ctx_len), lambda b, j: (0, ctx_blk0 + b))]
    args += [k, vT]
    kern = functools.partial(_attn_kernel, n_lat_chunks=n_lat_chunks, lam_init=lam_init, tq=tq)
    return pl.pallas_call(
        kern,
        grid=(n_batch, n_q),
        in_specs=in_specs,
        out_specs=pl.BlockSpec((tq, DA_W), o_map),
        out_shape=jax.ShapeDtypeStruct((out_rows, DA_W), BF16),
        scratch_shapes=[pltpu.VMEM((n_keys, tq), F32)],
        compiler_params=_cparams(2),
        name="diff_attn_latent" if latent else "diff_attn_ctx",
    )(*args)


def _mix_kernel(*refs, n_x_tiles, tiles_per_seq, has_ctx, tm):
    if has_ctx:
        (x_ref, mod_ref, g_ref, pa_ref, pap_ref, pan_ref, pb_ref, hd_ref, hdp_ref, hdn_ref,
         ox_ref, oc_ref, *refs) = refs
    else:
        (x_ref, mod_ref, g_ref, pa_ref, pap_ref, pan_ref, pb_ref, hd_ref, hdp_ref, hdn_ref,
         ox_ref, *refs) = refs
        oc_ref = None
    (wg_ref, caw_ref, wa_ref, sw_ref, sb_ref, wb_ref, wc_ref, dw_ref, db_ref, lg_ref, lb_ref,
     wd_ref, wo_ref, out_ref, cxe, hde) = refs
    d = x_ref.shape[1]
    i = pl.program_id(0)
    in_seq = i % tiles_per_seq
    prev_zero = in_seq == 0
    next_zero = in_seq == tiles_per_seq - 1
    if has_ctx:
        is_ctx = i >= n_x_tiles
        prev_zero = jnp.logical_or(prev_zero, is_ctx)
        next_zero = jnp.logical_or(next_zero, is_ctx)

    x = x_ref[...]
    hx = _modnorm(x, g_ref[...], mod_ref[3:4, :], mod_ref[4:5, :]).astype(BF16)

    def fill(ext, main, prev, nxt):
        ext[HALO:HALO + tm, :] = main.astype(F32)
        ext[0:HALO, :] = jnp.where(prev_zero, 0.0, prev.astype(F32))
        ext[HALO + tm:2 * HALO + tm, :] = jnp.where(next_zero, 0.0, nxt.astype(F32))

    def dwconv(ext, w_ref, taps, r0, rows):
        acc = None
        for j in range(taps):
            term = w_ref[j:j + 1, :] * ext[pl.ds(HALO + r0 + j - taps // 2, rows), :]
            acc = term if acc is None else acc + term
        return acc

    rb = 128
    n_rb = tm // rb

    fill(cxe, pa_ref[:, CONV_A_W:], pap_ref[:, CONV_A_W:], pan_ref[:, CONV_A_W:])
    a_parts = [pa_ref[r * rb:(r + 1) * rb, 0:CONV_A_W].astype(F32) * dwconv(cxe, caw_ref, CONV_A_K, r * rb, rb)
               for r in range(n_rb)]
    a_in = jnp.concatenate(a_parts, axis=0).astype(BF16)

    lane = lax.broadcasted_iota(jnp.int32, (CHUNK, SGU_W), 1)
    b_parts = []
    for n in range(tm // CHUNK):
        rs = slice(n * CHUNK, (n + 1) * CHUNK)
        vn = pb_ref[rs, SGU_W:]
        vbd = jnp.concatenate(
            [jnp.where(lane // SGU_GROUP_W == gi, vn, jnp.zeros_like(vn)) for gi in range(SGU_GROUPS)], axis=0)
        s = _dot(sw_ref[...], vbd) + sb_ref[...]
        b_parts.append(pb_ref[rs, 0:SGU_W].astype(F32) * s)
    b_in = jnp.concatenate(b_parts, axis=0).astype(BF16)

    o = ox_ref[...]
    if has_ctx:
        o = jnp.where(is_ctx, oc_ref[...], o)

    fill(hde, hd_ref[...], hdp_ref[...], hdn_ref[...])
    d_parts = []
    for r in range(n_rb):
        hcv = dwconv(hde, dw_ref, CONF_K, r * rb, rb) + db_ref[...]
        mu = jnp.mean(hcv, axis=-1, keepdims=True)
        hc = hcv - mu
        hn = hc * lax.rsqrt(jnp.mean(hc * hc, axis=-1, keepdims=True) + EPS)
        hn = hn * lg_ref[...] + lb_ref[...]
        d_parts.append(hn * jax.nn.sigmoid(hn))
    d_in = jnp.concatenate(d_parts, axis=0).astype(BF16)

    def gate(j):
        return jax.nn.sigmoid(_dot(hx, wg_ref[:, j * d:(j + 1) * d]))

    merged = gate(0) * _dot(a_in, wa_ref[...])
    merged = merged + gate(1) * _dot(b_in, wb_ref[...])
    merged = merged + gate(2) * _dot(o, wc_ref[...])
    merged = merged + gate(3) * _dot(d_in, wd_ref[...])
    y = _dot(merged.astype(BF16), wo_ref[...])
    out_ref[...] = x + mod_ref[5:6, :] * y


def _mix_call(xs, mods, layer, g, pa, pb, hd, o_x, o_c, wts, *, n_batch, seq, ctx_len, n_out_rows):
    ntok, d = xs.shape
    tm = TM_MIX
    assert ctx_len == tm and seq % tm == 0
    n_x_tiles = n_batch * seq // tm
    tiles_per_seq = seq // tm
    n_tiles = n_out_rows // tm
    has_ctx = o_c is not None
    hpt = tm // HALO
    n_halo_blocks = ntok // HALO
    row = lambda i: (i, 0)
    prev = lambda i: (jnp.maximum(i * hpt - 1, 0), 0)
    nxt = lambda i: (jnp.minimum((i + 1) * hpt, n_halo_blocks - 1), 0)
    in_specs = [
        pl.BlockSpec((tm, d), row),
        pl.BlockSpec((None, None, N_MOD, d),
                     lambda i: (layer, _mod_row(i, n_x_tiles, tiles_per_seq, n_batch), 0, 0)),
        _resident((1, d)),
        pl.BlockSpec((tm, 2 * CONV_A_W), row),
        pl.BlockSpec((HALO, 2 * CONV_A_W), prev),
        pl.BlockSpec((HALO, 2 * CONV_A_W), nxt),
        pl.BlockSpec((tm, 2 * SGU_W), row),
        pl.BlockSpec((tm, CONF_W), row),
        pl.BlockSpec((HALO, CONF_W), prev),
        pl.BlockSpec((HALO, CONF_W), nxt),
    ]
    args = [xs, mods, g, pa, pa, pa, pb, hd, hd, hd]
    if has_ctx:
        in_specs += [pl.BlockSpec((tm, DA_W), lambda i: (jnp.minimum(i, n_x_tiles - 1), 0)),
                     pl.BlockSpec((tm, DA_W), lambda i: (jnp.maximum(i - n_x_tiles, 0), 0))]
        args += [o_x, o_c]
    else:
        in_specs.append(pl.BlockSpec((tm, DA_W), row))
        args.append(o_x)
    in_specs += [_resident(w.shape) for w in wts]
    args += list(wts)
    kern = functools.partial(_mix_kernel, n_x_tiles=n_x_tiles, tiles_per_seq=tiles_per_seq,
                             has_ctx=has_ctx, tm=tm)
    return pl.pallas_call(
        kern,
        grid=(n_tiles,),
        in_specs=in_specs,
        out_specs=pl.BlockSpec((tm, d), row),
        out_shape=jax.ShapeDtypeStruct((n_out_rows, d), F32),
        scratch_shapes=[pltpu.VMEM((tm + 2 * HALO, CONV_A_W), F32),
                        pltpu.VMEM((tm + 2 * HALO, CONF_W), F32)],
        compiler_params=_cparams(1),
        name="mixer_merge",
    )(*args)


def _rope_tables(n_batch, seq, ctx_len):
    n_rows = seq // GRID_W
    row = jnp.repeat(jnp.arange(n_rows, dtype=F32), GRID_W)
    col = jnp.tile(jnp.arange(GRID_W, dtype=F32), n_rows)
    inv = ROPE_BASE ** (-jnp.arange(ROPE_NF, dtype=F32) / ROPE_NF)
    ar = row[:, None] * inv
    ac = col[:, None] * inv
    cos64 = jnp.concatenate([jnp.cos(ar), jnp.cos(ar), jnp.cos(ac), jnp.cos(ac)], axis=1)
    sin64 = jnp.concatenate([-jnp.sin(ar), jnp.sin(ar), -jnp.sin(ac), jnp.sin(ac)], axis=1)
    cos_l = jnp.tile(jnp.concatenate([cos64, cos64], axis=1), (n_batch, 1))
    sin_l = jnp.tile(jnp.concatenate([sin64, sin64], axis=1), (n_batch, 1))
    n_c = n_batch * ctx_len
    cos_t = jnp.concatenate([cos_l, jnp.ones((n_c, HEAD_W), F32)], axis=0)
    sin_t = jnp.concatenate([sin_l, jnp.zeros((n_c, HEAD_W), F32)], axis=0)
    return cos_t, sin_t


def kernel(x, c, ctx, c_ctx, w_ada, b_ada, norm_g, ffn1_w1, ffn1_w3, ffn1_w2, ffn2_w1, ffn2_w3, ffn2_w2, w_in, conv_a_w, w_a_out, sgu_w, sgu_b, w_b_out, lam_p, subln_g, w_c_out, conf_dw, conf_db, conf_ln_g, conf_ln_b, w_d_out, w_o, final_g):
    n_batch, seq, d = x.shape
    ctx_len = ctx.shape[1]
    depth = w_ada.shape[0]
    n_x = n_batch * seq
    n_c = n_batch * ctx_len
    ntok = n_x + n_c
    assert n_batch + 1 <= MOD_ROWS and seq % TM_TOK == 0 and n_c % TM_TOK == 0 and seq % TQ == 0
    assert seq % GRID_W == 0 and w_in.shape[2] == G_OFF + N_BRANCH * d
    dims = dict(n_batch=n_batch, seq=seq)

    bf = lambda a: a.astype(BF16)
    cvec = jnp.zeros((MOD_ROWS, d), F32).at[:n_batch].set(c).at[n_batch].set(c_ctx)
    mods = _ada_call(cvec, w_ada, b_ada).reshape(depth, MOD_ROWS, N_MOD, d)
    cos_t, sin_t = _rope_tables(n_batch, seq, ctx_len)

    xs = None
    for l in range(depth):
        last = l == depth - 1
        lam_init = 0.8 - 0.6 * float(np.exp(-0.3 * l))
        g = norm_g[l]
        if l == 0:
            xs = _ffn_call(x.reshape(n_x, d), ctx.reshape(n_c, d), mods, l, 0, g[0:1], bf(ffn1_w1[l]),
                           bf(ffn1_w3[l]), bf(ffn1_w2[l]), None, n_out_rows=ntok, **dims)
        else:
            xs = _ffn_call(xs, None, mods, l, 0, g[0:1], bf(ffn1_w1[l]), bf(ffn1_w3[l]), bf(ffn1_w2[l]),
                           None, n_out_rows=ntok, **dims)
        pa, pb, qT, k, vT, hd = _proj_call(xs, mods, l, g[1:2], bf(w_in[l][:, :G_OFF]), cos_t, sin_t, **dims)
        sg_col = subln_g[l].reshape(HEAD_W, 1)
        o_x = _attn_call(lam_p[l], sg_col, qT, k, vT, lam_init, ctx_len=ctx_len, latent=True, **dims)
        o_c = None
        if not last:
            o_c = _attn_call(lam_p[l], sg_col, qT, k, vT, lam_init, ctx_len=ctx_len, latent=False, **dims)
        sw_cat = bf(jnp.transpose(sgu_w[l], (1, 0, 2)).reshape(CHUNK, SGU_GROUPS * CHUNK))
        sb_full = jnp.repeat(sgu_b[l].T, SGU_GROUP_W, axis=1)
        wts = (bf(w_in[l][:, G_OFF:]), conv_a_w[l], bf(w_a_out[l]), sw_cat, sb_full, bf(w_b_out[l]),
               bf(w_c_out[l]), conf_dw[l], conf_db[l].reshape(1, CONF_W), conf_ln_g[l].reshape(1, CONF_W),
               conf_ln_b[l].reshape(1, CONF_W), bf(w_d_out[l]), bf(w_o[l]))
        rows = n_x if last else ntok
        xs = _mix_call(xs, mods, l, g[1:2], pa, pb, hd, o_x, o_c, wts, ctx_len=ctx_len, n_out_rows=rows, **dims)
        xs = _ffn_call(xs, None, mods, l, 6, g[2:3], bf(ffn2_w1[l]), bf(ffn2_w3[l]), bf(ffn2_w2[l]),
                       final_g.reshape(1, d) if last else None, n_out_rows=rows, **dims)
    return xs.reshape(n_batch, seq, d)
```

```python
import functools
import math

import jax
import jax.numpy as jnp
import numpy as np
from jax import lax
from jax.experimental import pallas as pl
from jax.experimental.pallas import tpu as pltpu

F32 = jnp.float32
BF16 = jnp.bfloat16

EPS = 1e-6
GRID_W = 64
N_MOD = 9
ROPE_BASE = 10000.0

CONV_A_W = 256
CONV_A_K = 3
SGU_W = 256
SGU_GROUPS = 4
SGU_GROUP_W = SGU_W // SGU_GROUPS
CHUNK = 128
DA_HEADS = 4
DA_HEAD_DIM = 64
HEAD_W = 2 * DA_HEAD_DIM
DA_W = DA_HEADS * HEAD_W
ROPE_NF = DA_HEAD_DIM // 4
CONF_W = 256
CONF_K = 31
N_BRANCH = 4
Q_SCALE = DA_HEAD_DIM ** -0.5 * math.log2(math.e)

A_OFF = 0
B_OFF = A_OFF + 3 * CONV_A_W
Q_OFF = B_OFF + 2 * SGU_W
K_OFF = Q_OFF + DA_W
V_OFF = K_OFF + DA_W
D_OFF = V_OFF + DA_W
G_OFF = D_OFF + 2 * CONF_W

TM_TOK = 512
TM_MIX = 256
TQ = 512
KC = 256
HALO = 16
MOD_ROWS = 8
VMEM_LIMIT = 56 * 1024 * 1024


def _cparams(n_grid):
    return pltpu.CompilerParams(dimension_semantics=("arbitrary",) * n_grid,
                                vmem_limit_bytes=VMEM_LIMIT)


def _resident(shape):
    nd = len(shape)
    return pl.BlockSpec(shape, lambda *_: (0,) * nd, pipeline_mode=pl.Buffered(1))


def _dot(a, b):
    return jnp.dot(a, b, preferred_element_type=F32)


def _modnorm(x, g, shift, scale):
    y = x * lax.rsqrt(jnp.mean(x * x, axis=-1, keepdims=True) + EPS)
    return (y * g) * (1.0 + scale) + shift


def _ada_kernel(c_ref, w_ref, b_ref, o_ref):
    c = c_ref[...]
    s = (c * jax.nn.sigmoid(c)).astype(BF16)
    o_ref[...] = _dot(s, w_ref[...].astype(BF16)) + b_ref[...]


def _ada_call(cvec, w_ada, b_ada):
    depth, d, n = w_ada.shape
    tn = math.gcd(n, 1536)
    return pl.pallas_call(
        _ada_kernel,
        grid=(depth, n // tn),
        in_specs=[
            pl.BlockSpec((MOD_ROWS, d), lambda l, j: (0, 0)),
            pl.BlockSpec((None, d, tn), lambda l, j: (l, 0, j)),
            pl.BlockSpec((None, 1, tn), lambda l, j: (l, 0, j)),
        ],
        out_specs=pl.BlockSpec((None, MOD_ROWS, tn), lambda l, j: (l, 0, j)),
        out_shape=jax.ShapeDtypeStruct((depth, MOD_ROWS, n), F32),
        compiler_params=_cparams(2),
        name="ada_mod",
    )(cvec, w_ada, b_ada.reshape(depth, 1, n))


def _ffn_chunks(d_ff):
    tiles = d_ff // 256
    assert tiles * 256 == d_ff and tiles >= 2
    cut = (tiles + 1) // 2 * 256
    return ((0, cut), (cut, d_ff))


def _ffn_kernel(*refs, n_x_tiles, two_inputs, final, mod_base, d_ff):
    if two_inputs:
        x_ref, c_ref, *refs = refs
    else:
        x_ref, *refs = refs
    if final:
        mod_ref, g_ref, w1_ref, w3_ref, w2_ref, fg_ref, o_ref = refs
    else:
        mod_ref, g_ref, w1_ref, w3_ref, w2_ref, o_ref = refs
    x = x_ref[...]
    if two_inputs:
        x = jnp.where(pl.program_id(0) < n_x_tiles, x, c_ref[...])
    shift = mod_ref[mod_base:mod_base + 1, :]
    scale = mod_ref[mod_base + 1:mod_base + 2, :]
    gate = mod_ref[mod_base + 2:mod_base + 3, :]
    h = _modnorm(x, g_ref[...], shift, scale).astype(BF16)
    acc = None
    for lo, hi in _ffn_chunks(d_ff):
        a = _dot(h, w1_ref[:, lo:hi])
        b = _dot(h, w3_ref[:, lo:hi])
        gch = (a * jax.nn.sigmoid(a) * b).astype(BF16)
        part = _dot(gch, w2_ref[lo:hi, :])
        acc = part if acc is None else acc + part
    y = x + (0.5 * gate) * acc
    if final:
        y = y * lax.rsqrt(jnp.mean(y * y, axis=-1, keepdims=True) + EPS) * fg_ref[...]
    o_ref[...] = y


def _mod_row(i, n_x_tiles, tiles_per_batch, n_batch):
    return jnp.where(i < n_x_tiles, i // tiles_per_batch, n_batch)


def _ffn_call(xs, ctx2d, mods, layer, mod_base, g, w1, w3, w2, final_g, *, n_batch, seq, n_out_rows):
    d = xs.shape[1]
    d_ff = w1.shape[1]
    tm = TM_TOK
    n_x_tiles = n_batch * seq // tm
    n_tiles = n_out_rows // tm
    tiles_per_batch = seq // tm
    two_inputs = ctx2d is not None
    final = final_g is not None
    in_specs = []
    args = []
    if two_inputs:
        in_specs.append(pl.BlockSpec((tm, d), lambda i: (jnp.minimum(i, n_x_tiles - 1), 0)))
        in_specs.append(pl.BlockSpec((tm, d), lambda i: (jnp.maximum(i - n_x_tiles, 0), 0)))
        args += [xs, ctx2d]
    else:
        in_specs.append(pl.BlockSpec((tm, d), lambda i: (i, 0)))
        args.append(xs)
    in_specs += [
        pl.BlockSpec((None, None, N_MOD, d),
                     lambda i: (layer, _mod_row(i, n_x_tiles, tiles_per_batch, n_batch), 0, 0)),
        _resident((1, d)),
        _resident((d, d_ff)),
        _resident((d, d_ff)),
        _resident((d_ff, d)),
    ]
    args += [mods, g, w1, w3, w2]
    if final:
        in_specs.append(_resident((1, d)))
        args.append(final_g)
    kern = functools.partial(_ffn_kernel, n_x_tiles=n_x_tiles, two_inputs=two_inputs, final=final,
                             mod_base=mod_base, d_ff=d_ff)
    return pl.pallas_call(
        kern,
        grid=(n_tiles,),
        in_specs=in_specs,
        out_specs=pl.BlockSpec((tm, d), lambda i: (i, 0)),
        out_shape=jax.ShapeDtypeStruct((n_out_rows, d), F32),
        compiler_params=_cparams(1),
        name="ffn",
    )(*args)


def _gelu_tanh(x):
    c = math.sqrt(2.0 / math.pi)
    return x * (0.5 * (1.0 + jnp.tanh(c * (x + 0.044715 * (x * x * x)))))


def _proj_kernel(x_ref, mod_ref, g_ref, w_ref, cos_ref, sin_ref,
                 pa_ref, pb_ref, qT_ref, k_ref, vT_ref, hd_ref):
    x = x_ref[...]
    h = _modnorm(x, g_ref[...], mod_ref[3:4, :], mod_ref[4:5, :]).astype(BF16)

    def proj(lo, hi):
        return _dot(h, w_ref[:, lo:hi])

    pa = proj(A_OFF, B_OFF)
    pa_ref[:, 0:CONV_A_W] = pa[:, 0:CONV_A_W].astype(BF16)
    pa_ref[:, CONV_A_W:] = (pa[:, CONV_A_W:2 * CONV_A_W] * pa[:, 2 * CONV_A_W:]).astype(BF16)

    z = _gelu_tanh(proj(B_OFF, Q_OFF))
    v = z[:, SGU_W:]
    vc = v - jnp.mean(v, axis=-1, keepdims=True)
    vn = vc * lax.rsqrt(jnp.mean(vc * vc, axis=-1, keepdims=True) + EPS)
    pb_ref[:, 0:SGU_W] = z[:, 0:SGU_W].astype(BF16)
    pb_ref[:, SGU_W:] = vn.astype(BF16)

    cos = cos_ref[...]
    sin = sin_ref[...]
    lane = lax.broadcasted_iota(jnp.int32, cos.shape, 1)
    first_half = (lane % (2 * ROPE_NF)) < ROPE_NF

    def rope(blk):
        partner = jnp.where(first_half, pltpu.roll(blk, HEAD_W - ROPE_NF, 1), pltpu.roll(blk, ROPE_NF, 1))
        return blk * cos + partner * sin

    q = proj(Q_OFF, K_OFF)
    for hh in range(DA_HEADS):
        sl = slice(hh * HEAD_W, (hh + 1) * HEAD_W)
        qr = rope(q[:, sl]) * Q_SCALE
        qT_ref[sl, :] = qr.T.astype(BF16)
    k = proj(K_OFF, V_OFF)
    for hh in range(DA_HEADS):
        sl = slice(hh * HEAD_W, (hh + 1) * HEAD_W)
        k_ref[:, sl] = rope(k[:, sl]).astype(BF16)
    vv = proj(V_OFF, D_OFF)
    for hh in range(DA_HEADS):
        sl = slice(hh * HEAD_W, (hh + 1) * HEAD_W)
        vT_ref[sl, :] = vv[:, sl].T.astype(BF16)

    zd = proj(D_OFF, G_OFF)
    hd_ref[...] = (zd[:, 0:CONF_W] * jax.nn.sigmoid(zd[:, CONF_W:])).astype(BF16)


def _proj_call(xs, mods, layer, g, w_small, cos_t, sin_t, *, n_batch, seq):
    ntok, d = xs.shape
    tm = TM_TOK
    n_x_tiles = n_batch * seq // tm
    tiles_per_batch = seq // tm
    row = lambda i: (i, 0)
    col = lambda i: (0, i)
    outs = pl.pallas_call(
        _proj_kernel,
        grid=(ntok // tm,),
        in_specs=[
            pl.BlockSpec((tm, d), row),
            pl.BlockSpec((None, None, N_MOD, d),
                         lambda i: (layer, _mod_row(i, n_x_tiles, tiles_per_batch, n_batch), 0, 0)),
            _resident((1, d)),
            _resident((d, G_OFF)),
            pl.BlockSpec((tm, HEAD_W), row),
            pl.BlockSpec((tm, HEAD_W), row),
        ],
        out_specs=[
            pl.BlockSpec((tm, 2 * CONV_A_W), row),
            pl.BlockSpec((tm, 2 * SGU_W), row),
            pl.BlockSpec((DA_W, tm), col),
            pl.BlockSpec((tm, DA_W), row),
            pl.BlockSpec((DA_W, tm), col),
            pl.BlockSpec((tm, CONF_W), row),
        ],
        out_shape=[
            jax.ShapeDtypeStruct((ntok, 2 * CONV_A_W), BF16),
            jax.ShapeDtypeStruct((ntok, 2 * SGU_W), BF16),
            jax.ShapeDtypeStruct((DA_W, ntok), BF16),
            jax.ShapeDtypeStruct((ntok, DA_W), BF16),
            jax.ShapeDtypeStruct((DA_W, ntok), BF16),
            jax.ShapeDtypeStruct((ntok, CONF_W), BF16),
        ],
        compiler_params=_cparams(1),
        name="mixer_proj",
    )(xs, mods, g, w_small, cos_t, sin_t)
    return outs


def _attn_kernel(*refs, n_lat_chunks, lam_init, tq):
    if n_lat_chunks:
        lam_ref, sg_ref, qT_ref, kl_ref, vl_ref, kc_ref, vc_ref, o_ref = refs
    else:
        lam_ref, sg_ref, qT_ref, kc_ref, vc_ref, o_ref = refs
        kl_ref = vl_ref = None

    lp = lam_ref[...]
    lam = (jnp.exp(jnp.sum(lp[0:1, :] * lp[1:2, :], axis=-1, keepdims=True))
           - jnp.exp(jnp.sum(lp[2:3, :] * lp[3:4, :], axis=-1, keepdims=True)) + lam_init)
    sub = lax.broadcasted_iota(jnp.int32, (HEAD_W, tq), 0)
    qh = qT_ref[...]
    zero = jnp.zeros_like(qh)
    qm = (jnp.where(sub < DA_HEAD_DIM, qh, zero),
          jnp.where(sub >= DA_HEAD_DIM, qh, zero))

    o_maps = []
    for m in range(2):
        s_ctx = _dot(kc_ref[...], qm[m])
        mx = jnp.max(s_ctx, axis=0, keepdims=True)
        if n_lat_chunks:
            s_lat = _dot(kl_ref[...], qm[m])
            mx = jnp.maximum(mx, jnp.max(s_lat, axis=0, keepdims=True))
        p_ctx = jnp.exp2(s_ctx - mx)
        lsum = jnp.sum(p_ctx, axis=0, keepdims=True)
        acc = _dot(vc_ref[...], p_ctx.astype(BF16))
        if n_lat_chunks:
            p_lat = jnp.exp2(s_lat - mx)
            lsum = lsum + jnp.sum(p_lat, axis=0, keepdims=True)
            acc = acc + _dot(vl_ref[...], p_lat.astype(BF16))
        o_maps.append(acc / lsum)

    oh = o_maps[0] - lam * o_maps[1]
    y = oh * lax.rsqrt(jnp.mean(oh * oh, axis=0, keepdims=True) + EPS)
    y = y * (sg_ref[...] * (1.0 - lam_init))
    o_ref[...] = y.T.astype(BF16)


def _attn_call(lam_p_l, sg_col, qT, k, vT, lam_init, *, n_batch, seq, ctx_len, latent):
    ntok = k.shape[0]
    n_x = n_batch * seq
    ctx_blk0 = n_x // ctx_len
    if latent:
        tq = TQ
        n_q = seq // tq
        n_lat_chunks = seq // KC
        q_map = lambda b, h, j: (h, b * n_q + j)
        o_map = lambda b, h, j: (b * n_q + j, h)
        out_rows = n_x
    else:
        tq = ctx_len
        n_q = 1
        n_lat_chunks = 0
        q_map = lambda b, h, j: (h, ctx_blk0 + b)
        o_map = lambda b, h, j: (b, h)
        out_rows = n_batch * ctx_len
    in_specs = [
        _resident(lam_p_l.shape),
        _resident(sg_col.shape),
        pl.BlockSpec((HEAD_W, tq), q_map),
    ]
    args = [lam_p_l, sg_col, qT]
    if latent:
        in_specs += [pl.BlockSpec((seq, HEAD_W), lambda b, h, j: (b, h)),
                     pl.BlockSpec((HEAD_W, seq), lambda b, h, j: (h, b))]
        args += [k, vT]
    in_specs += [pl.BlockSpec((ctx_len, HEAD_W), lambda b, h, j: (ctx_blk0 + b, h)),
                 pl.BlockSpec((HEAD_W, ctx_len), lambda b, h, j: (h, ctx_blk0 + b))]
    args += [k, vT]
    kern = functools.partial(_attn_kernel, n_lat_chunks=n_lat_chunks, lam_init=lam_init, tq=tq)
    return pl.pallas_call(
        kern,
        grid=(n_batch, DA_HEADS, n_q),
        in_specs=in_specs,
        out_specs=pl.BlockSpec((tq, HEAD_W), o_map),
        out_shape=jax.ShapeDtypeStruct((out_rows, DA_W), BF16),
        compiler_params=_cparams(3),
        name="diff_attn_latent" if latent else "diff_attn_ctx",
    )(*args)


def _mix_kernel(*refs, n_x_tiles, tiles_per_seq, has_ctx, tm):
    if has_ctx:
        (x_ref, mod_ref, g_ref, pa_ref, pap_ref, pan_ref, pb_ref, hd_ref, hdp_ref, hdn_ref,
         ox_ref, oc_ref, *refs) = refs
    else:
        (x_ref, mod_ref, g_ref, pa_ref, pap_ref, pan_ref, pb_ref, hd_ref, hdp_ref, hdn_ref,
         ox_ref, *refs) = refs
        oc_ref = None
    (wg_ref, caw_ref, wa_ref, sw_ref, sb_ref, wb_ref, wc_ref, dw_ref, db_ref, lg_ref, lb_ref,
     wd_ref, wo_ref, out_ref, cxe, hde) = refs
    d = x_ref.shape[1]
    i = pl.program_id(0)
    in_seq = i % tiles_per_seq
    prev_zero = in_seq == 0
    next_zero = in_seq == tiles_per_seq - 1
    if has_ctx:
        is_ctx = i >= n_x_tiles
        prev_zero = jnp.logical_or(prev_zero, is_ctx)
        next_zero = jnp.logical_or(next_zero, is_ctx)

    x = x_ref[...]
    hx = _modnorm(x, g_ref[...], mod_ref[3:4, :], mod_ref[4:5, :]).astype(BF16)

    def fill(ext, main, prev, nxt):
        ext[HALO:HALO + tm, :] = main.astype(F32)
        ext[0:HALO, :] = jnp.where(prev_zero, 0.0, prev.astype(F32))
        ext[HALO + tm:2 * HALO + tm, :] = jnp.where(next_zero, 0.0, nxt.astype(F32))

    def dwconv(ext, w_ref, taps, r0, rows):
        acc = None
        for j in range(taps):
            term = w_ref[j:j + 1, :] * ext[pl.ds(HALO + r0 + j - taps // 2, rows), :]
            acc = term if acc is None else acc + term
        return acc

    rb = 128
    n_rb = tm // rb

    fill(cxe, pa_ref[:, CONV_A_W:], pap_ref[:, CONV_A_W:], pan_ref[:, CONV_A_W:])
    a_parts = [pa_ref[r * rb:(r + 1) * rb, 0:CONV_A_W].astype(F32) * dwconv(cxe, caw_ref, CONV_A_K, r * rb, rb)
               for r in range(n_rb)]
    a_in = jnp.concatenate(a_parts, axis=0).astype(BF16)

    lane = lax.broadcasted_iota(jnp.int32, (CHUNK, SGU_W), 1)
    b_parts = []
    for n in range(tm // CHUNK):
        rs = slice(n * CHUNK, (n + 1) * CHUNK)
        vn = pb_ref[rs, SGU_W:]
        vbd = jnp.concatenate(
            [jnp.where(lane // SGU_GROUP_W == gi, vn, jnp.zeros_like(vn)) for gi in range(SGU_GROUPS)], axis=0)
        s = _dot(sw_ref[...], vbd) + sb_ref[...]
        b_parts.append(pb_ref[rs, 0:SGU_W].astype(F32) * s)
    b_in = jnp.concatenate(b_parts, axis=0).astype(BF16)

    o = ox_ref[...]
    if has_ctx:
        o = jnp.where(is_ctx, oc_ref[...], o)

    fill(hde, hd_ref[...], hdp_ref[...], hdn_ref[...])
    d_parts = []
    for r in range(n_rb):
        hcv = dwconv(hde, dw_ref, CONF_K, r * rb, rb) + db_ref[...]
        mu = jnp.mean(hcv, axis=-1, keepdims=True)
        hc = hcv - mu
        hn = hc * lax.rsqrt(jnp.mean(hc * hc, axis=-1, keepdims=True) + EPS)
        hn = hn * lg_ref[...] + lb_ref[...]
        d_parts.append(hn * jax.nn.sigmoid(hn))
    d_in = jnp.concatenate(d_parts, axis=0).astype(BF16)

    def gate(j):
        return jax.nn.sigmoid(_dot(hx, wg_ref[:, j * d:(j + 1) * d]))

    merged = gate(0) * _dot(a_in, wa_ref[...])
    merged = merged + gate(1) * _dot(b_in, wb_ref[...])
    merged = merged + gate(2) * _dot(o, wc_ref[...])
    merged = merged + gate(3) * _dot(d_in, wd_ref[...])
    y = _dot(merged.astype(BF16), wo_ref[...])
    out_ref[...] = x + mod_ref[5:6, :] * y


def _mix_call(xs, mods, layer, g, pa, pb, hd, o_x, o_c, wts, *, n_batch, seq, ctx_len, n_out_rows):
    ntok, d = xs.shape
    tm = TM_MIX
    assert ctx_len == tm and seq % tm == 0
    n_x_tiles = n_batch * seq // tm
    tiles_per_seq = seq // tm
    n_tiles = n_out_rows // tm
    has_ctx = o_c is not None
    hpt = tm // HALO
    n_halo_blocks = ntok // HALO
    row = lambda i: (i, 0)
    prev = lambda i: (jnp.maximum(i * hpt - 1, 0), 0)
    nxt = lambda i: (jnp.minimum((i + 1) * hpt, n_halo_blocks - 1), 0)
    in_specs = [
        pl.BlockSpec((tm, d), row),
        pl.BlockSpec((None, None, N_MOD, d),
                     lambda i: (layer, _mod_row(i, n_x_tiles, tiles_per_seq, n_batch), 0, 0)),
        _resident((1, d)),
        pl.BlockSpec((tm, 2 * CONV_A_W), row),
        pl.BlockSpec((HALO, 2 * CONV_A_W), prev),
        pl.BlockSpec((HALO, 2 * CONV_A_W), nxt),
        pl.BlockSpec((tm, 2 * SGU_W), row),
        pl.BlockSpec((tm, CONF_W), row),
        pl.BlockSpec((HALO, CONF_W), prev),
        pl.BlockSpec((HALO, CONF_W), nxt),
    ]
    args = [xs, mods, g, pa, pa, pa, pb, hd, hd, hd]
    if has_ctx:
        in_specs += [pl.BlockSpec((tm, DA_W), lambda i: (jnp.minimum(i, n_x_tiles - 1), 0)),
                     pl.BlockSpec((tm, DA_W), lambda i: (jnp.maximum(i - n_x_tiles, 0), 0))]
        args += [o_x, o_c]
    else:
        in_specs.append(pl.BlockSpec((tm, DA_W), row))
        args.append(o_x)
    in_specs += [_resident(w.shape) for w in wts]
    args += list(wts)
    kern = functools.partial(_mix_kernel, n_x_tiles=n_x_tiles, tiles_per_seq=tiles_per_seq,
                             has_ctx=has_ctx, tm=tm)
    return pl.pallas_call(
        kern,
        grid=(n_tiles,),
        in_specs=in_specs,
        out_specs=pl.BlockSpec((tm, d), row),
        out_shape=jax.ShapeDtypeStruct((n_out_rows, d), F32),
        scratch_shapes=[pltpu.VMEM((tm + 2 * HALO, CONV_A_W), F32),
                        pltpu.VMEM((tm + 2 * HALO, CONF_W), F32)],
        compiler_params=_cparams(1),
        name="mixer_merge",
    )(*args)


def _rope_tables(n_batch, seq, ctx_len):
    n_rows = seq // GRID_W
    row = jnp.repeat(jnp.arange(n_rows, dtype=F32), GRID_W)
    col = jnp.tile(jnp.arange(GRID_W, dtype=F32), n_rows)
    inv = ROPE_BASE ** (-jnp.arange(ROPE_NF, dtype=F32) / ROPE_NF)
    ar = row[:, None] * inv
    ac = col[:, None] * inv
    cos64 = jnp.concatenate([jnp.cos(ar), jnp.cos(ar), jnp.cos(ac), jnp.cos(ac)], axis=1)
    sin64 = jnp.concatenate([-jnp.sin(ar), jnp.sin(ar), -jnp.sin(ac), jnp.sin(ac)], axis=1)
    cos_l = jnp.tile(jnp.concatenate([cos64, cos64], axis=1), (n_batch, 1))
    sin_l = jnp.tile(jnp.concatenate([sin64, sin64], axis=1), (n_batch, 1))
    n_c = n_batch * ctx_len
    cos_t = jnp.concatenate([cos_l, jnp.ones((n_c, HEAD_W), F32)], axis=0)
    sin_t = jnp.concatenate([sin_l, jnp.zeros((n_c, HEAD_W), F32)], axis=0)
    return cos_t, sin_t


def kernel(x, c, ctx, c_ctx, w_ada, b_ada, norm_g, ffn1_w1, ffn1_w3, ffn1_w2, ffn2_w1, ffn2_w3, ffn2_w2, w_in, conv_a_w, w_a_out, sgu_w, sgu_b, w_b_out, lam_p, subln_g, w_c_out, conf_dw, conf_db, conf_ln_g, conf_ln_b, w_d_out, w_o, final_g):
    n_batch, seq, d = x.shape
    ctx_len = ctx.shape[1]
    depth = w_ada.shape[0]
    n_x = n_batch * seq
    n_c = n_batch * ctx_len
    ntok = n_x + n_c
    assert n_batch + 1 <= MOD_ROWS and seq % TM_TOK == 0 and n_c % TM_TOK == 0 and seq % TQ == 0
    assert seq % GRID_W == 0 and w_in.shape[2] == G_OFF + N_BRANCH * d
    dims = dict(n_batch=n_batch, seq=seq)

    bf = lambda a: a.astype(BF16)
    cvec = jnp.zeros((MOD_ROWS, d), F32).at[:n_batch].set(c).at[n_batch].set(c_ctx)
    mods = _ada_call(cvec, w_ada, b_ada).reshape(depth, MOD_ROWS, N_MOD, d)
    cos_t, sin_t = _rope_tables(n_batch, seq, ctx_len)

    xs = None
    for l in range(depth):
        last = l == depth - 1
        lam_init = 0.8 - 0.6 * float(np.exp(-0.3 * l))
        g = norm_g[l]
        if l == 0:
            xs = _ffn_call(x.reshape(n_x, d), ctx.reshape(n_c, d), mods, l, 0, g[0:1], bf(ffn1_w1[l]),
                           bf(ffn1_w3[l]), bf(ffn1_w2[l]), None, n_out_rows=ntok, **dims)
        else:
            xs = _ffn_call(xs, None, mods, l, 0, g[0:1], bf(ffn1_w1[l]), bf(ffn1_w3[l]), bf(ffn1_w2[l]),
                           None, n_out_rows=ntok, **dims)
        pa, pb, qT, k, vT, hd = _proj_call(xs, mods, l, g[1:2], bf(w_in[l][:, :G_OFF]), cos_t, sin_t, **dims)
        sg_col = subln_g[l].reshape(HEAD_W, 1)
        o_x = _attn_call(lam_p[l], sg_col, qT, k, vT, lam_init, ctx_len=ctx_len, latent=True, **dims)
        o_c = None
        if not last:
            o_c = _attn_call(lam_p[l], sg_col, qT, k, vT, lam_init, ctx_len=ctx_len, latent=False, **dims)
        sw_cat = bf(jnp.transpose(sgu_w[l], (1, 0, 2)).reshape(CHUNK, SGU_GROUPS * CHUNK))
        sb_full = jnp.repeat(sgu_b[l].T, SGU_GROUP_W, axis=1)
        wts = (bf(w_in[l][:, G_OFF:]), conv_a_w[l], bf(w_a_out[l]), sw_cat, sb_full, bf(w_b_out[l]),
               bf(w_c_out[l]), conf_dw[l], conf_db[l].reshape(1, CONF_W), conf_ln_g[l].reshape(1, CONF_W),
               conf_ln_b[l].reshape(1, CONF_W), bf(w_d_out[l]), bf(w_o[l]))
        rows = n_x if last else ntok
        xs = _mix_call(xs, mods, l, g[1:2], pa, pb, hd, o_x, o_c, wts, ctx_len=ctx_len, n_out_rows=rows, **dims)
        xs = _ffn_call(xs, None, mods, l, 6, g[2:3], bf(ffn2_w1[l]), bf(ffn2_w3[l]), bf(ffn2_w2[l]),
                       final_g.reshape(1, d) if last else None, n_out_rows=rows, **dims)
    return xs.reshape(n_batch, seq, d)
```

```python
import functools
import math

import jax
import jax.numpy as jnp
import numpy as np
from jax import lax
from jax.experimental import pallas as pl
from jax.experimental.pallas import tpu as pltpu

F32 = jnp.float32
BF16 = jnp.bfloat16

EPS = 1e-6
GRID_W = 64
N_MOD = 9
ROPE_BASE = 10000.0

CONV_A_W = 256
CONV_A_K = 3
SGU_W = 256
SGU_GROUPS = 4
SGU_GROUP_W = SGU_W // SGU_GROUPS
CHUNK = 128
DA_HEADS = 4
DA_HEAD_DIM = 64
HEAD_W = 2 * DA_HEAD_DIM
DA_W = DA_HEADS * HEAD_W
ROPE_NF = DA_HEAD_DIM // 4
CONF_W = 256
CONF_K = 31
N_BRANCH = 4
Q_SCALE = DA_HEAD_DIM ** -0.5 * math.log2(math.e)

A_OFF = 0
B_OFF = A_OFF + 3 * CONV_A_W
Q_OFF = B_OFF + 2 * SGU_W
K_OFF = Q_OFF + DA_W
V_OFF = K_OFF + DA_W
D_OFF = V_OFF + DA_W
G_OFF = D_OFF + 2 * CONF_W

SUBLANES = 8
TM_TOK = 512
TM_MIX = 256
TQ = 512
KB = 512
ATTN_HEADS_PER_STEP = 2
HALO = 16
CONV_RB = 128
MOD_ROWS = 8
VMEM_LIMIT = 56 * 1024 * 1024


def _cparams(n_grid, flags=None):
    return pltpu.CompilerParams(dimension_semantics=("arbitrary",) * n_grid,
                                vmem_limit_bytes=VMEM_LIMIT, flags=flags)


def _resident(shape):
    nd = len(shape)
    return pl.BlockSpec(shape, lambda *_: (0,) * nd, pipeline_mode=pl.Buffered(1))


def _layer_resident(arr, layer):
    nd = arr.ndim - 1
    return pl.BlockSpec((None,) + arr.shape[1:], lambda *_: (layer,) + (0,) * nd,
                        pipeline_mode=pl.Buffered(1))


def _dot(a, b):
    return jnp.dot(a, b, preferred_element_type=F32)


def _modnorm(x, g, shift, scale):
    y = x * lax.rsqrt(jnp.mean(x * x, axis=-1, keepdims=True) + EPS)
    return (y * g) * (1.0 + scale) + shift


def _ada_kernel(c_ref, w_ref, b_ref, o_ref):
    c = c_ref[...]
    s = (c * jax.nn.sigmoid(c)).astype(BF16)
    o_ref[...] = _dot(s, w_ref[...].astype(BF16)) + b_ref[...]


def _ada_call(cvec, w_ada, b_ada):
    depth, d, n = w_ada.shape
    tn = math.gcd(n, 1536)
    return pl.pallas_call(
        _ada_kernel,
        grid=(depth, n // tn),
        in_specs=[
            pl.BlockSpec((MOD_ROWS, d), lambda l, j: (0, 0)),
            pl.BlockSpec((None, d, tn), lambda l, j: (l, 0, j)),
            pl.BlockSpec((None, 1, tn), lambda l, j: (l, 0, j)),
        ],
        out_specs=pl.BlockSpec((None, MOD_ROWS, tn), lambda l, j: (l, 0, j)),
        out_shape=jax.ShapeDtypeStruct((depth, MOD_ROWS, n), F32),
        compiler_params=_cparams(2),
        name="ada_mod",
    )(cvec, w_ada, b_ada.reshape(depth, 1, n))


def _ffn_chunks(d_ff):
    tiles = d_ff // 256
    assert tiles * 256 == d_ff and tiles >= 2
    cut = (tiles + 1) // 2 * 256
    return ((0, cut), (cut, d_ff))


def _ffn_kernel(*refs, n_x_tiles, two_inputs, final, norm_row, mod_base, d_ff):
    if two_inputs:
        x_ref, c_ref, *refs = refs
    else:
        x_ref, *refs = refs
    if final:
        mod_ref, g_ref, w1_ref, w3_ref, w2_ref, fg_ref, o_ref = refs
    else:
        mod_ref, g_ref, w1_ref, w3_ref, w2_ref, o_ref = refs
    x = x_ref[...]
    if two_inputs:
        x = jnp.where(pl.program_id(0) < n_x_tiles, x, c_ref[...])
    shift = mod_ref[mod_base:mod_base + 1, :]
    scale = mod_ref[mod_base + 1:mod_base + 2, :]
    gate = mod_ref[mod_base + 2:mod_base + 3, :]
    h = _modnorm(x, g_ref[norm_row:norm_row + 1, :], shift, scale).astype(BF16)
    acc = None
    for lo, hi in _ffn_chunks(d_ff):
        a = _dot(h, w1_ref[:, lo:hi])
        b = _dot(h, w3_ref[:, lo:hi])
        gch = (a * jax.nn.sigmoid(a) * b).astype(BF16)
        part = _dot(gch, w2_ref[lo:hi, :])
        acc = part if acc is None else acc + part
    y = x + (0.5 * gate) * acc
    if final:
        y = y * lax.rsqrt(jnp.mean(y * y, axis=-1, keepdims=True) + EPS) * fg_ref[...]
    o_ref[...] = y


def _mod_row(i, n_x_tiles, tiles_per_batch, n_batch):
    return jnp.where(i < n_x_tiles, i // tiles_per_batch, n_batch)


def _ffn_call(xs, ctx2d, mods, layer, norm_row, mod_base, norm_g, w1, w3, w2, final_g, *,
              n_batch, seq, n_out_rows):
    d = xs.shape[1]
    d_ff = w1.shape[2]
    tm = TM_TOK
    n_x_tiles = n_batch * seq // tm
    n_tiles = n_out_rows // tm
    tiles_per_batch = seq // tm
    two_inputs = ctx2d is not None
    final = final_g is not None
    in_specs = []
    args = []
    if two_inputs:
        in_specs.append(pl.BlockSpec((tm, d), lambda i: (jnp.minimum(i, n_x_tiles - 1), 0)))
        in_specs.append(pl.BlockSpec((tm, d), lambda i: (jnp.maximum(i - n_x_tiles, 0), 0)))
        args += [xs, ctx2d]
    else:
        in_specs.append(pl.BlockSpec((tm, d), lambda i: (i, 0)))
        args.append(xs)
    in_specs += [
        pl.BlockSpec((None, None, N_MOD, d),
                     lambda i: (layer, _mod_row(i, n_x_tiles, tiles_per_batch, n_batch), 0, 0)),
        _layer_resident(norm_g, layer),
        _layer_resident(w1, layer),
        _layer_resident(w3, layer),
        _layer_resident(w2, layer),
    ]
    args += [mods, norm_g, w1, w3, w2]
    if final:
        in_specs.append(_resident((1, d)))
        args.append(final_g)
    kern = functools.partial(_ffn_kernel, n_x_tiles=n_x_tiles, two_inputs=two_inputs, final=final,
                             norm_row=norm_row, mod_base=mod_base, d_ff=d_ff)
    return pl.pallas_call(
        kern,
        grid=(n_tiles,),
        in_specs=in_specs,
        out_specs=pl.BlockSpec((tm, d), lambda i: (i, 0)),
        out_shape=jax.ShapeDtypeStruct((n_out_rows, d), F32),
        compiler_params=_cparams(1),
        name="ffn",
    )(*args)


def _gelu_tanh(x):
    c = math.sqrt(2.0 / math.pi)
    return x * (0.5 * (1.0 + jnp.tanh(c * (x + 0.044715 * (x * x * x)))))


def _proj_kernel(x_ref, mod_ref, g_ref, w_ref, cos_ref, sin_ref,
                 pa_ref, pb_ref, qT_ref, k_ref, vT_ref, hd_ref):
    x = x_ref[...]
    h = _modnorm(x, g_ref[1:2, :], mod_ref[3:4, :], mod_ref[4:5, :]).astype(BF16)

    def proj(lo, hi):
        return _dot(h, w_ref[:, lo:hi])

    pa = proj(A_OFF, B_OFF)
    pa_ref[:, 0:CONV_A_W] = pa[:, 0:CONV_A_W].astype(BF16)
    pa_ref[:, CONV_A_W:] = (pa[:, CONV_A_W:2 * CONV_A_W] * pa[:, 2 * CONV_A_W:]).astype(BF16)

    z = _gelu_tanh(proj(B_OFF, Q_OFF))
    v = z[:, SGU_W:]
    vc = v - jnp.mean(v, axis=-1, keepdims=True)
    vn = vc * lax.rsqrt(jnp.mean(vc * vc, axis=-1, keepdims=True) + EPS)
    pb_ref[:, 0:SGU_W] = z[:, 0:SGU_W].astype(BF16)
    pb_ref[:, SGU_W:] = vn.astype(BF16)

    cos = cos_ref[...]
    sin = sin_ref[...]
    lane = lax.broadcasted_iota(jnp.int32, cos.shape, 1)
    first_half = (lane % (2 * ROPE_NF)) < ROPE_NF

    def rope(blk):
        partner = jnp.where(first_half, pltpu.roll(blk, HEAD_W - ROPE_NF, 1), pltpu.roll(blk, ROPE_NF, 1))
        return blk * cos + partner * sin

    q = proj(Q_OFF, K_OFF)
    for hh in range(DA_HEADS):
        sl = slice(hh * HEAD_W, (hh + 1) * HEAD_W)
        qr = rope(q[:, sl]) * Q_SCALE
        qT_ref[sl, :] = qr.T.astype(BF16)
    k = proj(K_OFF, V_OFF)
    for hh in range(DA_HEADS):
        sl = slice(hh * HEAD_W, (hh + 1) * HEAD_W)
        k_ref[:, sl] = rope(k[:, sl]).astype(BF16)
    vv = proj(V_OFF, D_OFF)
    for hh in range(DA_HEADS):
        sl = slice(hh * HEAD_W, (hh + 1) * HEAD_W)
        vT_ref[sl, :] = vv[:, sl].T.astype(BF16)

    zd = proj(D_OFF, G_OFF)
    hd_ref[...] = (zd[:, 0:CONF_W] * jax.nn.sigmoid(zd[:, CONF_W:])).astype(BF16)


def _proj_call(xs, mods, layer, norm_g, w_in, cos_t, sin_t, *, n_batch, seq):
    ntok, d = xs.shape
    tm = TM_TOK
    n_x_tiles = n_batch * seq // tm
    tiles_per_batch = seq // tm
    row = lambda i: (i, 0)
    col = lambda i: (0, i)
    rope_row = lambda i: (jnp.where(i < n_x_tiles, i % tiles_per_batch, tiles_per_batch), 0)
    outs = pl.pallas_call(
        _proj_kernel,
        grid=(ntok // tm,),
        in_specs=[
            pl.BlockSpec((tm, d), row),
            pl.BlockSpec((None, None, N_MOD, d),
                         lambda i: (layer, _mod_row(i, n_x_tiles, tiles_per_batch, n_batch), 0, 0)),
            _layer_resident(norm_g, layer),
            pl.BlockSpec((None, d, G_OFF), lambda i: (layer, 0, 0), pipeline_mode=pl.Buffered(1)),
            pl.BlockSpec((tm, HEAD_W), rope_row),
            pl.BlockSpec((tm, HEAD_W), rope_row),
        ],
        out_specs=[
            pl.BlockSpec((tm, 2 * CONV_A_W), row),
            pl.BlockSpec((tm, 2 * SGU_W), row),
            pl.BlockSpec((DA_W, tm), col),
            pl.BlockSpec((tm, DA_W), row),
            pl.BlockSpec((DA_W, tm), col),
            pl.BlockSpec((tm, CONF_W), row),
        ],
        out_shape=[
            jax.ShapeDtypeStruct((ntok, 2 * CONV_A_W), BF16),
            jax.ShapeDtypeStruct((ntok, 2 * SGU_W), BF16),
            jax.ShapeDtypeStruct((DA_W, ntok), BF16),
            jax.ShapeDtypeStruct((ntok, DA_W), BF16),
            jax.ShapeDtypeStruct((DA_W, ntok), BF16),
            jax.ShapeDtypeStruct((ntok, CONF_W), BF16),
        ],
        compiler_params=_cparams(1),
        name="mixer_proj",
    )(xs, mods, norm_g, w_in, cos_t, sin_t)
    return outs


def _attn_kernel(*refs, has_latent, lam_init, tq):
    if has_latent:
        lam_ref, sg_ref, qT_ref, kl_ref, vl_ref, kc_ref, vc_ref, o_ref, s_scr = refs
    else:
        lam_ref, sg_ref, qT_ref, kc_ref, vc_ref, o_ref, s_scr = refs
    n_ctx = kc_ref.shape[0]
    blocks = []
    if has_latent:
        seq = kl_ref.shape[0]
        blocks += [(kl_ref, vl_ref, r0, KB, r0) for r0 in range(0, seq, KB)]
        blocks.append((kc_ref, vc_ref, 0, n_ctx, seq))
    else:
        blocks.append((kc_ref, vc_ref, 0, n_ctx, 0))

    lp = lam_ref[...]
    lam = (jnp.exp(jnp.sum(lp[0:1, :] * lp[1:2, :], axis=-1, keepdims=True))
           - jnp.exp(jnp.sum(lp[2:3, :] * lp[3:4, :], axis=-1, keepdims=True)) + lam_init)
    sub = lax.broadcasted_iota(jnp.int32, (HEAD_W, tq), 0)
    n_maps = 2 * (qT_ref.shape[0] // HEAD_W)

    def head_lanes(i):
        return slice((i // 2) * HEAD_W, (i // 2 + 1) * HEAD_W)

    def masked_q(i):
        qh = qT_ref[head_lanes(i), :]
        in_map = (sub < DA_HEAD_DIM) if i % 2 == 0 else (sub >= DA_HEAD_DIM)
        return jnp.where(in_map, qh, jnp.zeros_like(qh))

    def finish_head(hh, o0, o1):
        oh = o0 - lam * o1
        y = oh * lax.rsqrt(jnp.mean(oh * oh, axis=0, keepdims=True) + EPS)
        y = y * (sg_ref[...] * (1.0 - lam_init))
        o_ref[:, hh * HEAD_W:(hh + 1) * HEAD_W] = y.T.astype(BF16)

    mx_prev = None
    o_even = None
    for i in range(n_maps + 1):
        qm = masked_q(i) if i < n_maps else None
        mx_new = None
        lsum = acc = None
        for k_ref, v_ref, r0, rows, srow in blocks:
            if i < n_maps:
                s = _dot(k_ref[r0:r0 + rows, head_lanes(i)], qm)
                s_scr[i % 2, srow:srow + rows, :] = s
                bm = jnp.max(s, axis=0, keepdims=True)
                mx_new = bm if mx_new is None else jnp.maximum(mx_new, bm)
            if i > 0:
                p = jnp.exp2(s_scr[(i - 1) % 2, srow:srow + rows, :] - mx_prev)
                ps = jnp.sum(p, axis=0, keepdims=True)
                pv = _dot(v_ref[head_lanes(i - 1), r0:r0 + rows], p.astype(BF16))
                lsum = ps if lsum is None else lsum + ps
                acc = pv if acc is None else acc + pv
        if i > 0:
            o_map = acc / lsum
            if (i - 1) % 2 == 0:
                o_even = o_map
            else:
                finish_head((i - 1) // 2, o_even, o_map)
        mx_prev = mx_new


def _attn_call(lam_p, subln_col, layer, qT, k, vT, lam_init, *, n_batch, seq, ctx_len, latent):
    n_x = n_batch * seq
    hw = ATTN_HEADS_PER_STEP * HEAD_W
    ctx_blk0 = n_x // ctx_len
    if latent:
        tq = TQ
        n_q = seq // tq
        n_keys = seq + ctx_len
        q_map = lambda b, h, j: (h, b * n_q + j)
        o_map = lambda b, h, j: (b * n_q + j, h)
        out_rows = n_x
    else:
        tq = ctx_len
        n_q = 1
        n_keys = ctx_len
        q_map = lambda b, h, j: (h, ctx_blk0 + b)
        o_map = lambda b, h, j: (b, h)
        out_rows = n_batch * ctx_len
    in_specs = [
        _layer_resident(lam_p, layer),
        _layer_resident(subln_col, layer),
        pl.BlockSpec((hw, tq), q_map),
    ]
    args = [lam_p, subln_col, qT]
    if latent:
        in_specs += [pl.BlockSpec((seq, hw), lambda b, h, j: (b, h)),
                     pl.BlockSpec((hw, seq), lambda b, h, j: (h, b))]
        args += [k, vT]
    in_specs += [pl.BlockSpec((ctx_len, hw), lambda b, h, j: (ctx_blk0 + b, h)),
                 pl.BlockSpec((hw, ctx_len), lambda b, h, j: (h, ctx_blk0 + b))]
    args += [k, vT]
    kern = functools.partial(_attn_kernel, has_latent=latent, lam_init=lam_init, tq=tq)
    return pl.pallas_call(
        kern,
        grid=(n_batch, DA_HEADS // ATTN_HEADS_PER_STEP, n_q),
        in_specs=in_specs,
        out_specs=pl.BlockSpec((tq, hw), o_map),
        out_shape=jax.ShapeDtypeStruct((out_rows, DA_W), BF16),
        scratch_shapes=[pltpu.VMEM((2, n_keys, tq), F32)],
        compiler_params=_cparams(3),
        name="diff_attn_latent" if latent else "diff_attn_ctx",
    )(*args)


def _mix_kernel(*refs, n_x_tiles, tiles_per_seq, has_ctx, tm):
    if has_ctx:
        (x_ref, mod_ref, g_ref, pa_ref, pap_ref, pan_ref, pb_ref, hd_ref, hdp_ref, hdn_ref,
         ox_ref, oc_ref, *refs) = refs
    else:
        (x_ref, mod_ref, g_ref, pa_ref, pap_ref, pan_ref, pb_ref, hd_ref, hdp_ref, hdn_ref,
         ox_ref, *refs) = refs
        oc_ref = None
    (win_ref, caw_ref, wa_ref, sw_ref, sb_ref, wb_ref, wc_ref, dw_ref, db_ref, lg_ref, lb_ref,
     wd_ref, wo_ref, out_ref, cxe, hde, hsh, gsc, bin_scr, ysc) = refs
    d = x_ref.shape[1]
    i = pl.program_id(0)
    in_seq = i % tiles_per_seq
    prev_zero = in_seq == 0
    next_zero = in_seq == tiles_per_seq - 1
    if has_ctx:
        is_ctx = i >= n_x_tiles
        prev_zero = jnp.logical_or(prev_zero, is_ctx)
        next_zero = jnp.logical_or(next_zero, is_ctx)

    x = x_ref[...]
    hx = _modnorm(x, g_ref[1:2, :], mod_ref[3:4, :], mod_ref[4:5, :]).astype(BF16)
    def fill(ext, main, prev, nxt):
        ext[HALO:HALO + tm, :] = main.astype(F32)
        ext[0:HALO, :] = jnp.where(prev_zero, 0.0, prev.astype(F32))
        ext[HALO + tm:2 * HALO + tm, :] = jnp.where(next_zero, 0.0, nxt.astype(F32))

    def dwconv(ext, shifted, w_ref, taps, r0, rows):
        acc = None
        for j in range(taps):
            a, r = divmod(HALO + r0 + j - taps // 2, SUBLANES)
            if r == 0 or shifted is None:
                win = ext[pl.ds(a * SUBLANES + r, rows), :]
            else:
                win = shifted[r - 1, pl.ds(a * SUBLANES, rows), :]
            term = w_ref[j:j + 1, :] * win
            acc = term if acc is None else acc + term
        return acc

    def conv_a_task(r):
        def run():
            if r == 0:
                fill(cxe, pa_ref[:, CONV_A_W:], pap_ref[:, CONV_A_W:], pan_ref[:, CONV_A_W:])
            rs = slice(r * CONV_RB, (r + 1) * CONV_RB)
            y = pa_ref[rs, 0:CONV_A_W].astype(F32) * dwconv(cxe, None, caw_ref, CONV_A_K, r * CONV_RB, CONV_RB)
            bin_scr[0, rs, :] = y.astype(BF16)
        return run

    def sgu_task(n):
        def run():
            lane = lax.broadcasted_iota(jnp.int32, (CHUNK, SGU_W), 1)
            rs = slice(n * CHUNK, (n + 1) * CHUNK)
            vn = pb_ref[rs, SGU_W:]
            vbd = jnp.concatenate(
                [jnp.where(lane // SGU_GROUP_W == gi, vn, jnp.zeros_like(vn)) for gi in range(SGU_GROUPS)],
                axis=0)
            s = _dot(sw_ref[...], vbd) + sb_ref[...]
            bin_scr[1, rs, :] = (pb_ref[rs, 0:SGU_W].astype(F32) * s).astype(BF16)
        return run

    def conf_fill_task():
        fill(hde, hd_ref[...], hdp_ref[...], hdn_ref[...])
        n_sh = hsh.shape[1]
        for r in range(1, SUBLANES):
            hsh[r - 1, :, :] = hde[pl.ds(r, n_sh), :]

    def conf_task(r):
        def run():
            hcv = dwconv(hde, hsh, dw_ref, CONF_K, r * CONV_RB, CONV_RB) + db_ref[...]
            mu = jnp.mean(hcv, axis=-1, keepdims=True)
            hc = hcv - mu
            hn = hc * lax.rsqrt(jnp.mean(hc * hc, axis=-1, keepdims=True) + EPS)
            hn = hn * lg_ref[...] + lb_ref[...]
            bin_scr[2, r * CONV_RB:(r + 1) * CONV_RB, :] = (hn * jax.nn.sigmoid(hn)).astype(BF16)
        return run

    def gate_task(j, half):
        def run():
            cols = slice(half * (d // 2), (half + 1) * (d // 2))
            wcols = slice(G_OFF + j * d + cols.start, G_OFF + j * d + cols.stop)
            gsc[j, :, cols] = jax.nn.sigmoid(_dot(hx, win_ref[:, wcols]))
        return run

    n_rb = tm // CONV_RB
    assert n_rb == N_BRANCH - 2
    o = ox_ref[...]
    if has_ctx:
        o = jnp.where(is_ctx, oc_ref[...], o)

    for half in range(2):
        gate_task(0, half)()
    for half in range(2):
        gate_task(1, half)()
    for r in range(n_rb):
        conv_a_task(r)()
    ysc[0] = _dot(bin_scr[0], wa_ref[...])
    for n in range(tm // CHUNK):
        sgu_task(n)()
    ysc[1] = _dot(bin_scr[1], wb_ref[...])
    ysc[2] = _dot(o, wc_ref[...])
    conf_fill_task()
    for r in range(n_rb):
        for half in range(2):
            gate_task(2 + r, half)()
        conf_task(r)()
        rs = slice(r * CONV_RB, (r + 1) * CONV_RB)
        ysc[3, rs, :] = _dot(bin_scr[2, rs, :], wd_ref[...])

    merged = gsc[0] * ysc[0] + gsc[1] * ysc[1] + gsc[2] * ysc[2] + gsc[3] * ysc[3]
    y = _dot(merged.astype(BF16), wo_ref[...])
    out_ref[...] = x + mod_ref[5:6, :] * y


def _mix_call(xs, mods, layer, norm_g, pa, pb, hd, o_x, o_c, wts, *, n_batch, seq, ctx_len, n_out_rows):
    ntok, d = xs.shape
    tm = TM_MIX
    assert ctx_len == tm and seq % tm == 0
    n_x_tiles = n_batch * seq // tm
    tiles_per_seq = seq // tm
    n_tiles = n_out_rows // tm
    has_ctx = o_c is not None
    hpt = tm // HALO
    n_halo_blocks = ntok // HALO
    row = lambda i: (i, 0)
    prev = lambda i: (jnp.maximum(i * hpt - 1, 0), 0)
    nxt = lambda i: (jnp.minimum((i + 1) * hpt, n_halo_blocks - 1), 0)
    in_specs = [
        pl.BlockSpec((tm, d), row),
        pl.BlockSpec((None, None, N_MOD, d),
                     lambda i: (layer, _mod_row(i, n_x_tiles, tiles_per_seq, n_batch), 0, 0)),
        _layer_resident(norm_g, layer),
        pl.BlockSpec((tm, 2 * CONV_A_W), row),
        pl.BlockSpec((HALO, 2 * CONV_A_W), prev),
        pl.BlockSpec((HALO, 2 * CONV_A_W), nxt),
        pl.BlockSpec((tm, 2 * SGU_W), row),
        pl.BlockSpec((tm, CONF_W), row),
        pl.BlockSpec((HALO, CONF_W), prev),
        pl.BlockSpec((HALO, CONF_W), nxt),
    ]
    args = [xs, mods, norm_g, pa, pa, pa, pb, hd, hd, hd]
    if has_ctx:
        in_specs += [pl.BlockSpec((tm, DA_W), lambda i: (jnp.minimum(i, n_x_tiles - 1), 0)),
                     pl.BlockSpec((tm, DA_W), lambda i: (jnp.maximum(i - n_x_tiles, 0), 0))]
        args += [o_x, o_c]
    else:
        in_specs.append(pl.BlockSpec((tm, DA_W), row))
        args.append(o_x)
    in_specs += [_layer_resident(w, layer) for w in wts]
    args += list(wts)
    kern = functools.partial(_mix_kernel, n_x_tiles=n_x_tiles, tiles_per_seq=tiles_per_seq,
                             has_ctx=has_ctx, tm=tm)
    ext_rows = tm + 2 * HALO
    return pl.pallas_call(
        kern,
        grid=(n_tiles,),
        in_specs=in_specs,
        out_specs=pl.BlockSpec((tm, d), row),
        out_shape=jax.ShapeDtypeStruct((n_out_rows, d), F32),
        scratch_shapes=[pltpu.VMEM((ext_rows, CONV_A_W), F32),
                        pltpu.VMEM((ext_rows, CONF_W), F32),
                        pltpu.VMEM((SUBLANES - 1, ext_rows - SUBLANES, CONF_W), F32),
                        pltpu.VMEM((N_BRANCH, tm, d), F32),
                        pltpu.VMEM((N_BRANCH - 1, tm, CONV_A_W), BF16),
                        pltpu.VMEM((N_BRANCH, tm, d), F32)],
        compiler_params=_cparams(1),
        name="mixer_merge",
    )(*args)


def _rope_tables(seq, pad_rows):
    n_rows = seq // GRID_W
    row = np.repeat(np.arange(n_rows, dtype=np.float32), GRID_W)
    col = np.tile(np.arange(GRID_W, dtype=np.float32), n_rows)
    inv = (np.float32(ROPE_BASE) ** (-np.arange(ROPE_NF, dtype=np.float32) / np.float32(ROPE_NF))).astype(np.float32)
    ar = (row[:, None] * inv).astype(np.float32)
    ac = (col[:, None] * inv).astype(np.float32)
    cos64 = np.concatenate([np.cos(ar), np.cos(ar), np.cos(ac), np.cos(ac)], axis=1)
    sin64 = np.concatenate([-np.sin(ar), np.sin(ar), -np.sin(ac), np.sin(ac)], axis=1)
    cos_t = np.concatenate([cos64, cos64], axis=1)
    sin_t = np.concatenate([sin64, sin64], axis=1)
    cos_t = np.concatenate([cos_t, np.ones((pad_rows, HEAD_W), np.float32)], axis=0)
    sin_t = np.concatenate([sin_t, np.zeros((pad_rows, HEAD_W), np.float32)], axis=0)
    return jnp.asarray(cos_t, F32), jnp.asarray(sin_t, F32)


def kernel(x, c, ctx, c_ctx, w_ada, b_ada, norm_g, ffn1_w1, ffn1_w3, ffn1_w2, ffn2_w1, ffn2_w3, ffn2_w2, w_in, conv_a_w, w_a_out, sgu_w, sgu_b, w_b_out, lam_p, subln_g, w_c_out, conf_dw, conf_db, conf_ln_g, conf_ln_b, w_d_out, w_o, final_g):
    n_batch, seq, d = x.shape
    ctx_len = ctx.shape[1]
    depth = w_ada.shape[0]
    n_x = n_batch * seq
    n_c = n_batch * ctx_len
    ntok = n_x + n_c
    assert n_batch + 1 <= MOD_ROWS and seq % TM_TOK == 0 and n_c % TM_TOK == 0 and seq % TQ == 0
    assert seq % GRID_W == 0 and w_in.shape[2] == G_OFF + N_BRANCH * d
    dims = dict(n_batch=n_batch, seq=seq)

    bf = lambda a: a.astype(BF16)
    f1w1, f1w3, f1w2 = bf(ffn1_w1), bf(ffn1_w3), bf(ffn1_w2)
    f2w1, f2w3, f2w2 = bf(ffn2_w1), bf(ffn2_w3), bf(ffn2_w2)
    w_in_b = bf(w_in)
    sw_cat = bf(jnp.transpose(sgu_w, (0, 2, 1, 3)).reshape(depth, CHUNK, SGU_GROUPS * CHUNK))
    sb_full = jnp.repeat(jnp.transpose(sgu_b, (0, 2, 1)), SGU_GROUP_W, axis=2)
    mix_wts = (w_in_b, conv_a_w, bf(w_a_out), sw_cat, sb_full, bf(w_b_out), bf(w_c_out), conf_dw,
               conf_db.reshape(depth, 1, CONF_W), conf_ln_g.reshape(depth, 1, CONF_W),
               conf_ln_b.reshape(depth, 1, CONF_W), bf(w_d_out), bf(w_o))
    subln_col = subln_g.reshape(depth, HEAD_W, 1)

    cvec = jnp.zeros((MOD_ROWS, d), F32).at[:n_batch].set(c).at[n_batch].set(c_ctx)
    mods = _ada_call(cvec, w_ada, b_ada).reshape(depth, MOD_ROWS, N_MOD, d)
    cos_t, sin_t = _rope_tables(seq, TM_TOK)

    xs = None
    for l in range(depth):
        last = l == depth - 1
        lam_init = 0.8 - 0.6 * float(np.exp(-0.3 * l))
        if l == 0:
            xs = _ffn_call(x.reshape(n_x, d), ctx.reshape(n_c, d), mods, l, 0, 0, norm_g, f1w1, f1w3, f1w2,
                           None, n_out_rows=ntok, **dims)
        else:
            xs = _ffn_call(xs, None, mods, l, 0, 0, norm_g, f1w1, f1w3, f1w2, None, n_out_rows=ntok, **dims)
        pa, pb, qT, k, vT, hd = _proj_call(xs, mods, l, norm_g, w_in_b, cos_t, sin_t, **dims)
        o_x = _attn_call(lam_p, subln_col, l, qT, k, vT, lam_init, ctx_len=ctx_len, latent=True, **dims)
        o_c = None
        if not last:
            o_c = _attn_call(lam_p, subln_col, l, qT, k, vT, lam_init, ctx_len=ctx_len, latent=False, **dims)
        rows = n_x if last else ntok
        xs = _mix_call(xs, mods, l, norm_g, pa, pb, hd, o_x, o_c, mix_wts, ctx_len=ctx_len,
                       n_out_rows=rows, **dims)
        xs = _ffn_call(xs, None, mods, l, 2, 6, norm_g, f2w1, f2w3, f2w2,
                       final_g.reshape(1, d) if last else None, n_out_rows=rows, **dims)
    return xs.reshape(n_batch, seq, d)
```

```python
import functools
import math

import jax
import jax.numpy as jnp
import numpy as np
from jax import lax
from jax.experimental import pallas as pl
from jax.experimental.pallas import tpu as pltpu

F32 = jnp.float32
BF16 = jnp.bfloat16

EPS = 1e-6
GRID_W = 64
N_MOD = 9
ROPE_BASE = 10000.0

CONV_A_W = 256
CONV_A_K = 3
SGU_W = 256
SGU_GROUPS = 4
SGU_GROUP_W = SGU_W // SGU_GROUPS
CHUNK = 128
DA_HEADS = 4
DA_HEAD_DIM = 64
HEAD_W = 2 * DA_HEAD_DIM
DA_W = DA_HEADS * HEAD_W
ROPE_NF = DA_HEAD_DIM // 4
CONF_W = 256
CONF_K = 31
N_BRANCH = 4
Q_SCALE = DA_HEAD_DIM ** -0.5 * math.log2(math.e)

A_OFF = 0
B_OFF = A_OFF + 3 * CONV_A_W
Q_OFF = B_OFF + 2 * SGU_W
K_OFF = Q_OFF + DA_W
V_OFF = K_OFF + DA_W
D_OFF = V_OFF + DA_W
G_OFF = D_OFF + 2 * CONF_W

SUBLANES = 8
TM_TOK = 512
TM_MIX = 256
TQ = 512
KB = 512
ATTN_HEADS_PER_STEP = 2
HALO = 16
CONV_RB = 128
MOD_ROWS = 8
VMEM_LIMIT = 56 * 1024 * 1024


def _cparams(n_grid, flags=None):
    return pltpu.CompilerParams(dimension_semantics=("arbitrary",) * n_grid,
                                vmem_limit_bytes=VMEM_LIMIT, flags=flags)


def _resident(shape):
    nd = len(shape)
    return pl.BlockSpec(shape, lambda *_: (0,) * nd, pipeline_mode=pl.Buffered(1))


def _layer_resident(arr, layer):
    nd = arr.ndim - 1
    return pl.BlockSpec((None,) + arr.shape[1:], lambda *_: (layer,) + (0,) * nd,
                        pipeline_mode=pl.Buffered(1))


def _dot(a, b):
    return jnp.dot(a, b, preferred_element_type=F32)


def _modnorm(x, g, shift, scale):
    y = x * lax.rsqrt(jnp.mean(x * x, axis=-1, keepdims=True) + EPS)
    return (y * g) * (1.0 + scale) + shift


def _ada_kernel(c_ref, w_ref, b_ref, o_ref):
    c = c_ref[...]
    s = (c * jax.nn.sigmoid(c)).astype(BF16)
    o_ref[...] = _dot(s, w_ref[...].astype(BF16)) + b_ref[...]


def _ada_call(cvec, w_ada, b_ada):
    depth, d, n = w_ada.shape
    tn = math.gcd(n, 1536)
    return pl.pallas_call(
        _ada_kernel,
        grid=(depth, n // tn),
        in_specs=[
            pl.BlockSpec((MOD_ROWS, d), lambda l, j: (0, 0)),
            pl.BlockSpec((None, d, tn), lambda l, j: (l, 0, j)),
            pl.BlockSpec((None, 1, tn), lambda l, j: (l, 0, j)),
        ],
        out_specs=pl.BlockSpec((None, MOD_ROWS, tn), lambda l, j: (l, 0, j)),
        out_shape=jax.ShapeDtypeStruct((depth, MOD_ROWS, n), F32),
        compiler_params=_cparams(2),
        name="ada_mod",
    )(cvec, w_ada, b_ada.reshape(depth, 1, n))


def _ffn_chunks(d_ff):
    tiles = d_ff // 256
    assert tiles * 256 == d_ff and tiles >= 2
    cut = (tiles + 1) // 2 * 256
    return ((0, cut), (cut, d_ff))


def _ffn_kernel(*refs, n_x_tiles, two_inputs, final, norm_row, mod_base, d_ff):
    if two_inputs:
        x_ref, c_ref, *refs = refs
    else:
        x_ref, *refs = refs
    if final:
        mod_ref, g_ref, w1_ref, w3_ref, w2_ref, fg_ref, o_ref = refs
    else:
        mod_ref, g_ref, w1_ref, w3_ref, w2_ref, o_ref = refs
    x = x_ref[...]
    if two_inputs:
        x = jnp.where(pl.program_id(0) < n_x_tiles, x, c_ref[...])
    shift = mod_ref[mod_base:mod_base + 1, :]
    scale = mod_ref[mod_base + 1:mod_base + 2, :]
    gate = mod_ref[mod_base + 2:mod_base + 3, :]
    h = _modnorm(x, g_ref[norm_row:norm_row + 1, :], shift, scale).astype(BF16)
    acc = None
    for lo, hi in _ffn_chunks(d_ff):
        a = _dot(h, w1_ref[:, lo:hi])
        b = _dot(h, w3_ref[:, lo:hi])
        gch = (a * jax.nn.sigmoid(a) * b).astype(BF16)
        part = _dot(gch, w2_ref[lo:hi, :])
        acc = part if acc is None else acc + part
    y = x + (0.5 * gate) * acc
    if final:
        y = y * lax.rsqrt(jnp.mean(y * y, axis=-1, keepdims=True) + EPS) * fg_ref[...]
    o_ref[...] = y


def _mod_row(i, n_x_tiles, tiles_per_batch, n_batch):
    return jnp.where(i < n_x_tiles, i // tiles_per_batch, n_batch)


def _ffn_call(xs, ctx2d, mods, layer, norm_row, mod_base, norm_g, w1, w3, w2, final_g, *,
              n_batch, seq, n_out_rows):
    d = xs.shape[1]
    d_ff = w1.shape[2]
    tm = TM_TOK
    n_x_tiles = n_batch * seq // tm
    n_tiles = n_out_rows // tm
    tiles_per_batch = seq // tm
    two_inputs = ctx2d is not None
    final = final_g is not None
    in_specs = []
    args = []
    if two_inputs:
        in_specs.append(pl.BlockSpec((tm, d), lambda i: (jnp.minimum(i, n_x_tiles - 1), 0)))
        in_specs.append(pl.BlockSpec((tm, d), lambda i: (jnp.maximum(i - n_x_tiles, 0), 0)))
        args += [xs, ctx2d]
    else:
        in_specs.append(pl.BlockSpec((tm, d), lambda i: (i, 0)))
        args.append(xs)
    in_specs += [
        pl.BlockSpec((None, None, N_MOD, d),
                     lambda i: (layer, _mod_row(i, n_x_tiles, tiles_per_batch, n_batch), 0, 0)),
        _layer_resident(norm_g, layer),
        _layer_resident(w1, layer),
        _layer_resident(w3, layer),
        _layer_resident(w2, layer),
    ]
    args += [mods, norm_g, w1, w3, w2]
    if final:
        in_specs.append(_resident((1, d)))
        args.append(final_g)
    kern = functools.partial(_ffn_kernel, n_x_tiles=n_x_tiles, two_inputs=two_inputs, final=final,
                             norm_row=norm_row, mod_base=mod_base, d_ff=d_ff)
    return pl.pallas_call(
        kern,
        grid=(n_tiles,),
        in_specs=in_specs,
        out_specs=pl.BlockSpec((tm, d), lambda i: (i, 0)),
        out_shape=jax.ShapeDtypeStruct((n_out_rows, d), F32),
        compiler_params=_cparams(1),
        name="ffn",
    )(*args)


def _gelu_tanh(x):
    c = math.sqrt(2.0 / math.pi)
    return x * (0.5 * (1.0 + jnp.tanh(c * (x + 0.044715 * (x * x * x)))))


def _proj_kernel(x_ref, mod_ref, g_ref, w_ref, cos_ref, sin_ref,
                 hx_ref, pa_ref, pb_ref, qT_ref, k_ref, vT_ref, hd_ref):
    x = x_ref[...]
    h = _modnorm(x, g_ref[1:2, :], mod_ref[3:4, :], mod_ref[4:5, :]).astype(BF16)
    hx_ref[...] = h

    def proj(lo, hi):
        return _dot(h, w_ref[:, lo:hi])

    pa = proj(A_OFF, B_OFF)
    pa_ref[:, 0:CONV_A_W] = pa[:, 0:CONV_A_W].astype(BF16)
    pa_ref[:, CONV_A_W:] = (pa[:, CONV_A_W:2 * CONV_A_W] * pa[:, 2 * CONV_A_W:]).astype(BF16)

    z = _gelu_tanh(proj(B_OFF, Q_OFF))
    v = z[:, SGU_W:]
    vc = v - jnp.mean(v, axis=-1, keepdims=True)
    vn = vc * lax.rsqrt(jnp.mean(vc * vc, axis=-1, keepdims=True) + EPS)
    pb_ref[:, 0:SGU_W] = z[:, 0:SGU_W].astype(BF16)
    pb_ref[:, SGU_W:] = vn.astype(BF16)

    cos = cos_ref[...]
    sin = sin_ref[...]
    lane = lax.broadcasted_iota(jnp.int32, cos.shape, 1)
    first_half = (lane % (2 * ROPE_NF)) < ROPE_NF

    def rope(blk):
        partner = jnp.where(first_half, pltpu.roll(blk, HEAD_W - ROPE_NF, 1), pltpu.roll(blk, ROPE_NF, 1))
        return blk * cos + partner * sin

    q = proj(Q_OFF, K_OFF)
    for hh in range(DA_HEADS):
        sl = slice(hh * HEAD_W, (hh + 1) * HEAD_W)
        qr = rope(q[:, sl]) * Q_SCALE
        qT_ref[sl, :] = qr.T.astype(BF16)
    k = proj(K_OFF, V_OFF)
    for hh in range(DA_HEADS):
        sl = slice(hh * HEAD_W, (hh + 1) * HEAD_W)
        k_ref[:, sl] = rope(k[:, sl]).astype(BF16)
    vv = proj(V_OFF, D_OFF)
    for hh in range(DA_HEADS):
        sl = slice(hh * HEAD_W, (hh + 1) * HEAD_W)
        vT_ref[sl, :] = vv[:, sl].T.astype(BF16)

    zd = proj(D_OFF, G_OFF)
    hd_ref[...] = (zd[:, 0:CONF_W] * jax.nn.sigmoid(zd[:, CONF_W:])).astype(BF16)


def _proj_call(xs, mods, layer, norm_g, w_in, cos_t, sin_t, *, n_batch, seq):
    ntok, d = xs.shape
    tm = TM_TOK
    n_x_tiles = n_batch * seq // tm
    tiles_per_batch = seq // tm
    row = lambda i: (i, 0)
    col = lambda i: (0, i)
    rope_row = lambda i: (jnp.where(i < n_x_tiles, i % tiles_per_batch, tiles_per_batch), 0)
    outs = pl.pallas_call(
        _proj_kernel,
        grid=(ntok // tm,),
        in_specs=[
            pl.BlockSpec((tm, d), row),
            pl.BlockSpec((None, None, N_MOD, d),
                         lambda i: (layer, _mod_row(i, n_x_tiles, tiles_per_batch, n_batch), 0, 0)),
            _layer_resident(norm_g, layer),
            pl.BlockSpec((None, d, G_OFF), lambda i: (layer, 0, 0), pipeline_mode=pl.Buffered(1)),
            pl.BlockSpec((tm, HEAD_W), rope_row),
            pl.BlockSpec((tm, HEAD_W), rope_row),
        ],
        out_specs=[
            pl.BlockSpec((tm, d), row),
            pl.BlockSpec((tm, 2 * CONV_A_W), row),
            pl.BlockSpec((tm, 2 * SGU_W), row),
            pl.BlockSpec((DA_W, tm), col),
            pl.BlockSpec((tm, DA_W), row),
            pl.BlockSpec((DA_W, tm), col),
            pl.BlockSpec((tm, CONF_W), row),
        ],
        out_shape=[
            jax.ShapeDtypeStruct((ntok, d), BF16),
            jax.ShapeDtypeStruct((ntok, 2 * CONV_A_W), BF16),
            jax.ShapeDtypeStruct((ntok, 2 * SGU_W), BF16),
            jax.ShapeDtypeStruct((DA_W, ntok), BF16),
            jax.ShapeDtypeStruct((ntok, DA_W), BF16),
            jax.ShapeDtypeStruct((DA_W, ntok), BF16),
            jax.ShapeDtypeStruct((ntok, CONF_W), BF16),
        ],
        compiler_params=_cparams(1),
        name="mixer_proj",
    )(xs, mods, norm_g, w_in, cos_t, sin_t)
    return outs


def _attn_kernel(*refs, has_latent, lam_init, tq):
    if has_latent:
        lam_ref, sg_ref, qT_ref, kl_ref, vl_ref, kc_ref, vc_ref, o_ref, s_scr = refs
    else:
        lam_ref, sg_ref, qT_ref, kc_ref, vc_ref, o_ref, s_scr = refs
    n_ctx = kc_ref.shape[0]
    blocks = []
    if has_latent:
        seq = kl_ref.shape[0]
        blocks += [(kl_ref, vl_ref, r0, KB, r0) for r0 in range(0, seq, KB)]
        blocks.append((kc_ref, vc_ref, 0, n_ctx, seq))
    else:
        blocks.append((kc_ref, vc_ref, 0, n_ctx, 0))

    lp = lam_ref[...]
    lam = (jnp.exp(jnp.sum(lp[0:1, :] * lp[1:2, :], axis=-1, keepdims=True))
           - jnp.exp(jnp.sum(lp[2:3, :] * lp[3:4, :], axis=-1, keepdims=True)) + lam_init)
    sub = lax.broadcasted_iota(jnp.int32, (HEAD_W, tq), 0)
    n_maps = 2 * (qT_ref.shape[0] // HEAD_W)

    def head_lanes(i):
        return slice((i // 2) * HEAD_W, (i // 2 + 1) * HEAD_W)

    def masked_q(i):
        qh = qT_ref[head_lanes(i), :]
        in_map = (sub < DA_HEAD_DIM) if i % 2 == 0 else (sub >= DA_HEAD_DIM)
        return jnp.where(in_map, qh, jnp.zeros_like(qh))

    def finish_head(hh, o0, o1):
        oh = o0 - lam * o1
        y = oh * lax.rsqrt(jnp.mean(oh * oh, axis=0, keepdims=True) + EPS)
        y = y * (sg_ref[...] * (1.0 - lam_init))
        o_ref[:, hh * HEAD_W:(hh + 1) * HEAD_W] = y.T.astype(BF16)

    mx_prev = None
    o_even = None
    for i in range(n_maps + 1):
        qm = masked_q(i) if i < n_maps else None
        mx_new = None
        lsum = acc = None
        for k_ref, v_ref, r0, rows, srow in blocks:
            if i < n_maps:
                s = _dot(k_ref[r0:r0 + rows, head_lanes(i)], qm)
                s_scr[i % 2, srow:srow + rows, :] = s
                bm = jnp.max(s, axis=0, keepdims=True)
                mx_new = bm if mx_new is None else jnp.maximum(mx_new, bm)
            if i > 0:
                p = jnp.exp2(s_scr[(i - 1) % 2, srow:srow + rows, :] - mx_prev)
                ps = jnp.sum(p, axis=0, keepdims=True)
                pv = _dot(v_ref[head_lanes(i - 1), r0:r0 + rows], p.astype(BF16))
                lsum = ps if lsum is None else lsum + ps
                acc = pv if acc is None else acc + pv
        if i > 0:
            o_map = acc / lsum
            if (i - 1) % 2 == 0:
                o_even = o_map
            else:
                finish_head((i - 1) // 2, o_even, o_map)
        mx_prev = mx_new


def _attn_call(lam_p, subln_col, layer, qT, k, vT, lam_init, *, n_batch, seq, ctx_len, latent):
    n_x = n_batch * seq
    hw = ATTN_HEADS_PER_STEP * HEAD_W
    ctx_blk0 = n_x // ctx_len
    if latent:
        tq = TQ
        n_q = seq // tq
        n_keys = seq + ctx_len
        q_map = lambda b, h, j: (h, b * n_q + j)
        o_map = lambda b, h, j: (b * n_q + j, h)
        out_rows = n_x
    else:
        tq = ctx_len
        n_q = 1
        n_keys = ctx_len
        q_map = lambda b, h, j: (h, ctx_blk0 + b)
        o_map = lambda b, h, j: (b, h)
        out_rows = n_batch * ctx_len
    in_specs = [
        _layer_resident(lam_p, layer),
        _layer_resident(subln_col, layer),
        pl.BlockSpec((hw, tq), q_map),
    ]
    args = [lam_p, subln_col, qT]
    if latent:
        in_specs += [pl.BlockSpec((seq, hw), lambda b, h, j: (b, h)),
                     pl.BlockSpec((hw, seq), lambda b, h, j: (h, b))]
        args += [k, vT]
    in_specs += [pl.BlockSpec((ctx_len, hw), lambda b, h, j: (ctx_blk0 + b, h)),
                 pl.BlockSpec((hw, ctx_len), lambda b, h, j: (h, ctx_blk0 + b))]
    args += [k, vT]
    kern = functools.partial(_attn_kernel, has_latent=latent, lam_init=lam_init, tq=tq)
    return pl.pallas_call(
        kern,
        grid=(n_batch, DA_HEADS // ATTN_HEADS_PER_STEP, n_q),
        in_specs=in_specs,
        out_specs=pl.BlockSpec((tq, hw), o_map),
        out_shape=jax.ShapeDtypeStruct((out_rows, DA_W), BF16),
        scratch_shapes=[pltpu.VMEM((2, n_keys, tq), F32)],
        compiler_params=_cparams(3),
        name="diff_attn_latent" if latent else "diff_attn_ctx",
    )(*args)


def _attn_pipe_kernel(lam_ref, sg_ref, qT_ref, kl_ref, kc_ref, vl_ref, vc_ref, o_ref, s_scr, mx_scr, *,
                      lam_init, tq):
    step = pl.program_id(0)
    seq = kl_ref.shape[0]
    n_ctx = kc_ref.shape[0]
    blocks = [(kl_ref, vl_ref, r0, KB, r0) for r0 in range(0, seq, KB)]
    blocks.append((kc_ref, vc_ref, 0, n_ctx, seq))

    @pl.when(step == 0)
    def _():
        s_scr[1] = jnp.zeros(s_scr.shape[1:], F32)
        mx_scr[1] = jnp.zeros(mx_scr.shape[1:], F32)

    lp = lam_ref[...]
    lam = (jnp.exp(jnp.sum(lp[0:1, :] * lp[1:2, :], axis=-1, keepdims=True))
           - jnp.exp(jnp.sum(lp[2:3, :] * lp[3:4, :], axis=-1, keepdims=True)) + lam_init)

    def body(cur, prv):
        sub = lax.broadcasted_iota(jnp.int32, (HEAD_W, tq), 0)
        qh = qT_ref[...]
        o_maps = []
        for m in range(2):
            in_map = (sub < DA_HEAD_DIM) if m == 0 else (sub >= DA_HEAD_DIM)
            qm = jnp.where(in_map, qh, jnp.zeros_like(qh))
            mx_prev = mx_scr[prv, m, 0:1, :]
            mx_new = None
            lsum = acc = None
            for k_ref, v_ref, r0, rows, srow in blocks:
                s = _dot(k_ref[r0:r0 + rows, :], qm)
                s_scr[cur, m, srow:srow + rows, :] = s
                bm = jnp.max(s, axis=0, keepdims=True)
                mx_new = bm if mx_new is None else jnp.maximum(mx_new, bm)
                p = jnp.exp2(s_scr[prv, m, srow:srow + rows, :] - mx_prev)
                ps = jnp.sum(p, axis=0, keepdims=True)
                pv = _dot(v_ref[:, r0:r0 + rows], p.astype(BF16))
                lsum = ps if lsum is None else lsum + ps
                acc = pv if acc is None else acc + pv
            mx_scr[cur, m] = jnp.broadcast_to(mx_new, (SUBLANES, tq))
            o_maps.append(acc / lsum)
        oh = o_maps[0] - lam * o_maps[1]
        y = oh * lax.rsqrt(jnp.mean(oh * oh, axis=0, keepdims=True) + EPS)
        y = y * (sg_ref[...] * (1.0 - lam_init))
        o_ref[...] = y.T.astype(BF16)

    for cur in range(2):
        pl.when(step % 2 == cur)(functools.partial(body, cur, 1 - cur))


def _attn_pipe_call(lam_p, subln_col, layer, qT, k, vT, lam_init, *, n_batch, seq, ctx_len):
    n_x = n_batch * seq
    ctx_blk0 = n_x // ctx_len
    tq = TQ
    n_q = seq // tq
    total = n_batch * DA_HEADS * n_q

    def unit(t):
        return t // (DA_HEADS * n_q), (t // n_q) % DA_HEADS, t % n_q

    def cur(s):
        return unit(jnp.minimum(s, total - 1))

    def prv(s):
        return unit(jnp.maximum(s - 1, 0))

    def q_map(s):
        b, h, j = cur(s)
        return h, b * n_q + j

    def o_map(s):
        b, h, j = prv(s)
        return b * n_q + j, h

    in_specs = [
        _layer_resident(lam_p, layer),
        _layer_resident(subln_col, layer),
        pl.BlockSpec((HEAD_W, tq), q_map),
        pl.BlockSpec((seq, HEAD_W), lambda s: (cur(s)[0], cur(s)[1])),
        pl.BlockSpec((ctx_len, HEAD_W), lambda s: (ctx_blk0 + cur(s)[0], cur(s)[1])),
        pl.BlockSpec((HEAD_W, seq), lambda s: (prv(s)[1], prv(s)[0])),
        pl.BlockSpec((HEAD_W, ctx_len), lambda s: (prv(s)[1], ctx_blk0 + prv(s)[0])),
    ]
    n_keys = seq + ctx_len
    kern = functools.partial(_attn_pipe_kernel, lam_init=lam_init, tq=tq)
    return pl.pallas_call(
        kern,
        grid=(total + 1,),
        in_specs=in_specs,
        out_specs=pl.BlockSpec((tq, HEAD_W), o_map),
        out_shape=jax.ShapeDtypeStruct((n_x, DA_W), BF16),
        scratch_shapes=[pltpu.VMEM((2, 2, n_keys, tq), F32),
                        pltpu.VMEM((2, 2, SUBLANES, tq), F32)],
        compiler_params=_cparams(1),
        name="diff_attn_latent",
    )(lam_p, subln_col, qT, k, k, vT, vT)


def _mix_kernel(*refs, n_x_tiles, tiles_per_seq, has_ctx, tm):
    if has_ctx:
        (x_ref, mod_ref, hx_ref, pa_ref, pap_ref, pan_ref, pb_ref, hd_ref, hdp_ref, hdn_ref,
         ox_ref, oc_ref, *refs) = refs
    else:
        (x_ref, mod_ref, hx_ref, pa_ref, pap_ref, pan_ref, pb_ref, hd_ref, hdp_ref, hdn_ref,
         ox_ref, *refs) = refs
        oc_ref = None
    (win_ref, caw_ref, wa_ref, sw_ref, sb_ref, wb_ref, wc_ref, dw_ref, db_ref, lg_ref, lb_ref,
     wd_ref, wo_ref, out_ref, cxe, hde, hsh, gsc, bin_scr, ysc) = refs
    d = x_ref.shape[1]
    i = pl.program_id(0)
    in_seq = i % tiles_per_seq
    prev_zero = in_seq == 0
    next_zero = in_seq == tiles_per_seq - 1
    if has_ctx:
        is_ctx = i >= n_x_tiles
        prev_zero = jnp.logical_or(prev_zero, is_ctx)
        next_zero = jnp.logical_or(next_zero, is_ctx)

    x = x_ref[...]
    hx = hx_ref[...]

    def fill(ext, main, prev, nxt):
        ext[HALO:HALO + tm, :] = main.astype(F32)
        ext[0:HALO, :] = jnp.where(prev_zero, 0.0, prev.astype(F32))
        ext[HALO + tm:2 * HALO + tm, :] = jnp.where(next_zero, 0.0, nxt.astype(F32))

    def dwconv(ext, shifted, w_ref, taps, r0, rows):
        acc = None
        for j in range(taps):
            a, r = divmod(HALO + r0 + j - taps // 2, SUBLANES)
            if r == 0 or shifted is None:
                win = ext[pl.ds(a * SUBLANES + r, rows), :]
            else:
                win = shifted[r - 1, pl.ds(a * SUBLANES, rows), :]
            term = w_ref[j:j + 1, :] * win
            acc = term if acc is None else acc + term
        return acc

    def conv_a_task(r):
        def run():
            if r == 0:
                fill(cxe, pa_ref[:, CONV_A_W:], pap_ref[:, CONV_A_W:], pan_ref[:, CONV_A_W:])
            rs = slice(r * CONV_RB, (r + 1) * CONV_RB)
            y = pa_ref[rs, 0:CONV_A_W].astype(F32) * dwconv(cxe, None, caw_ref, CONV_A_K, r * CONV_RB, CONV_RB)
            bin_scr[0, rs, :] = y.astype(BF16)
        return run

    def sgu_task(n):
        def run():
            lane = lax.broadcasted_iota(jnp.int32, (CHUNK, SGU_W), 1)
            rs = slice(n * CHUNK, (n + 1) * CHUNK)
            vn = pb_ref[rs, SGU_W:]
            vbd = jnp.concatenate(
                [jnp.where(lane // SGU_GROUP_W == gi, vn, jnp.zeros_like(vn)) for gi in range(SGU_GROUPS)],
                axis=0)
            s = _dot(sw_ref[...], vbd) + sb_ref[...]
            bin_scr[1, rs, :] = (pb_ref[rs, 0:SGU_W].astype(F32) * s).astype(BF16)
        return run

    def conf_fill_task():
        fill(hde, hd_ref[...], hdp_ref[...], hdn_ref[...])
        n_sh = hsh.shape[1]
        for r in range(1, SUBLANES):
            hsh[r - 1, :, :] = hde[pl.ds(r, n_sh), :]

    def conf_task(r):
        def run():
            hcv = dwconv(hde, hsh, dw_ref, CONF_K, r * CONV_RB, CONV_RB) + db_ref[...]
            mu = jnp.mean(hcv, axis=-1, keepdims=True)
            hc = hcv - mu
            hn = hc * lax.rsqrt(jnp.mean(hc * hc, axis=-1, keepdims=True) + EPS)
            hn = hn * lg_ref[...] + lb_ref[...]
            bin_scr[2, r * CONV_RB:(r + 1) * CONV_RB, :] = (hn * jax.nn.sigmoid(hn)).astype(BF16)
        return run

    def gate_task(j, half):
        def run():
            cols = slice(half * (d // 2), (half + 1) * (d // 2))
            wcols = slice(G_OFF + j * d + cols.start, G_OFF + j * d + cols.stop)
            gsc[j, :, cols] = jax.nn.sigmoid(_dot(hx, win_ref[:, wcols]))
        return run

    n_rb = tm // CONV_RB
    assert n_rb == N_BRANCH - 2
    o = ox_ref[...]
    if has_ctx:
        o = jnp.where(is_ctx, oc_ref[...], o)

    for half in range(2):
        gate_task(0, half)()
    for half in range(2):
        gate_task(1, half)()
    for r in range(n_rb):
        conv_a_task(r)()
    ysc[0] = _dot(bin_scr[0], wa_ref[...])
    for n in range(tm // CHUNK):
        sgu_task(n)()
    ysc[1] = _dot(bin_scr[1], wb_ref[...])
    ysc[2] = _dot(o, wc_ref[...])
    conf_fill_task()
    for r in range(n_rb):
        for half in range(2):
            gate_task(2 + r, half)()
        conf_task(r)()
        rs = slice(r * CONV_RB, (r + 1) * CONV_RB)
        ysc[3, rs, :] = _dot(bin_scr[2, rs, :], wd_ref[...])

    merged = gsc[0] * ysc[0] + gsc[1] * ysc[1] + gsc[2] * ysc[2] + gsc[3] * ysc[3]
    y = _dot(merged.astype(BF16), wo_ref[...])
    out_ref[...] = x + mod_ref[5:6, :] * y


def _mix_call(xs, mods, layer, hx, pa, pb, hd, o_x, o_c, wts, *, n_batch, seq, ctx_len, n_out_rows):
    ntok, d = xs.shape
    tm = TM_MIX
    assert ctx_len == tm and seq % tm == 0
    n_x_tiles = n_batch * seq // tm
    tiles_per_seq = seq // tm
    n_tiles = n_out_rows // tm
    has_ctx = o_c is not None
    hpt = tm // HALO
    n_halo_blocks = ntok // HALO
    row = lambda i: (i, 0)
    prev = lambda i: (jnp.maximum(i * hpt - 1, 0), 0)
    nxt = lambda i: (jnp.minimum((i + 1) * hpt, n_halo_blocks - 1), 0)
    in_specs = [
        pl.BlockSpec((tm, d), row),
        pl.BlockSpec((None, None, N_MOD, d),
                     lambda i: (layer, _mod_row(i, n_x_tiles, tiles_per_seq, n_batch), 0, 0)),
        pl.BlockSpec((tm, d), row),
        pl.BlockSpec((tm, 2 * CONV_A_W), row),
        pl.BlockSpec((HALO, 2 * CONV_A_W), prev),
        pl.BlockSpec((HALO, 2 * CONV_A_W), nxt),
        pl.BlockSpec((tm, 2 * SGU_W), row),
        pl.BlockSpec((tm, CONF_W), row),
        pl.BlockSpec((HALO, CONF_W), prev),
        pl.BlockSpec((HALO, CONF_W), nxt),
    ]
    args = [xs, mods, hx, pa, pa, pa, pb, hd, hd, hd]
    if has_ctx:
        in_specs += [pl.BlockSpec((tm, DA_W), lambda i: (jnp.minimum(i, n_x_tiles - 1), 0)),
                     pl.BlockSpec((tm, DA_W), lambda i: (jnp.maximum(i - n_x_tiles, 0), 0))]
        args += [o_x, o_c]
    else:
        in_specs.append(pl.BlockSpec((tm, DA_W), row))
        args.append(o_x)
    in_specs += [_layer_resident(w, layer) for w in wts]
    args += list(wts)
    kern = functools.partial(_mix_kernel, n_x_tiles=n_x_tiles, tiles_per_seq=tiles_per_seq,
                             has_ctx=has_ctx, tm=tm)
    ext_rows = tm + 2 * HALO
    return pl.pallas_call(
        kern,
        grid=(n_tiles,),
        in_specs=in_specs,
        out_specs=pl.BlockSpec((tm, d), row),
        out_shape=jax.ShapeDtypeStruct((n_out_rows, d), F32),
        scratch_shapes=[pltpu.VMEM((ext_rows, CONV_A_W), F32),
                        pltpu.VMEM((ext_rows, CONF_W), F32),
                        pltpu.VMEM((SUBLANES - 1, ext_rows - SUBLANES, CONF_W), F32),
                        pltpu.VMEM((N_BRANCH, tm, d), F32),
                        pltpu.VMEM((N_BRANCH - 1, tm, CONV_A_W), BF16),
                        pltpu.VMEM((N_BRANCH, tm, d), F32)],
        compiler_params=_cparams(1),
        name="mixer_merge",
    )(*args)


def _rope_tables(seq, pad_rows):
    n_rows = seq // GRID_W
    row = np.repeat(np.arange(n_rows, dtype=np.float32), GRID_W)
    col = np.tile(np.arange(GRID_W, dtype=np.float32), n_rows)
    inv = (np.float32(ROPE_BASE) ** (-np.arange(ROPE_NF, dtype=np.float32) / np.float32(ROPE_NF))).astype(np.float32)
    ar = (row[:, None] * inv).astype(np.float32)
    ac = (col[:, None] * inv).astype(np.float32)
    cos64 = np.concatenate([np.cos(ar), np.cos(ar), np.cos(ac), np.cos(ac)], axis=1)
    sin64 = np.concatenate([-np.sin(ar), np.sin(ar), -np.sin(ac), np.sin(ac)], axis=1)
    cos_t = np.concatenate([cos64, cos64], axis=1)
    sin_t = np.concatenate([sin64, sin64], axis=1)
    cos_t = np.concatenate([cos_t, np.ones((pad_rows, HEAD_W), np.float32)], axis=0)
    sin_t = np.concatenate([sin_t, np.zeros((pad_rows, HEAD_W), np.float32)], axis=0)
    return jnp.asarray(cos_t, F32), jnp.asarray(sin_t, F32)


def kernel(x, c, ctx, c_ctx, w_ada, b_ada, norm_g, ffn1_w1, ffn1_w3, ffn1_w2, ffn2_w1, ffn2_w3, ffn2_w2, w_in, conv_a_w, w_a_out, sgu_w, sgu_b, w_b_out, lam_p, subln_g, w_c_out, conf_dw, conf_db, conf_ln_g, conf_ln_b, w_d_out, w_o, final_g):
    n_batch, seq, d = x.shape
    ctx_len = ctx.shape[1]
    depth = w_ada.shape[0]
    n_x = n_batch * seq
    n_c = n_batch * ctx_len
    ntok = n_x + n_c
    assert n_batch + 1 <= MOD_ROWS and seq % TM_TOK == 0 and n_c % TM_TOK == 0 and seq % TQ == 0
    assert seq % GRID_W == 0 and w_in.shape[2] == G_OFF + N_BRANCH * d
    dims = dict(n_batch=n_batch, seq=seq)

    bf = lambda a: a.astype(BF16)
    f1w1, f1w3, f1w2 = bf(ffn1_w1), bf(ffn1_w3), bf(ffn1_w2)
    f2w1, f2w3, f2w2 = bf(ffn2_w1), bf(ffn2_w3), bf(ffn2_w2)
    w_in_b = bf(w_in)
    sw_cat = bf(jnp.transpose(sgu_w, (0, 2, 1, 3)).reshape(depth, CHUNK, SGU_GROUPS * CHUNK))
    sb_full = jnp.repeat(jnp.transpose(sgu_b, (0, 2, 1)), SGU_GROUP_W, axis=2)
    mix_wts = (w_in_b, conv_a_w, bf(w_a_out), sw_cat, sb_full, bf(w_b_out), bf(w_c_out), conf_dw,
               conf_db.reshape(depth, 1, CONF_W), conf_ln_g.reshape(depth, 1, CONF_W),
               conf_ln_b.reshape(depth, 1, CONF_W), bf(w_d_out), bf(w_o))
    subln_col = subln_g.reshape(depth, HEAD_W, 1)

    cvec = jnp.zeros((MOD_ROWS, d), F32).at[:n_batch].set(c).at[n_batch].set(c_ctx)
    mods = _ada_call(cvec, w_ada, b_ada).reshape(depth, MOD_ROWS, N_MOD, d)
    cos_t, sin_t = _rope_tables(seq, TM_TOK)

    xs = None
    for l in range(depth):
        last = l == depth - 1
        lam_init = 0.8 - 0.6 * float(np.exp(-0.3 * l))
        if l == 0:
            xs = _ffn_call(x.reshape(n_x, d), ctx.reshape(n_c, d), mods, l, 0, 0, norm_g, f1w1, f1w3, f1w2,
                           None, n_out_rows=ntok, **dims)
        else:
            xs = _ffn_call(xs, None, mods, l, 0, 0, norm_g, f1w1, f1w3, f1w2, None, n_out_rows=ntok, **dims)
        hx, pa, pb, qT, k, vT, hd = _proj_call(xs, mods, l, norm_g, w_in_b, cos_t, sin_t, **dims)
        o_x = _attn_pipe_call(lam_p, subln_col, l, qT, k, vT, lam_init, ctx_len=ctx_len, **dims)
        o_c = None
        if not last:
            o_c = _attn_call(lam_p, subln_col, l, qT, k, vT, lam_init, ctx_len=ctx_len, latent=False, **dims)
        rows = n_x if last else ntok
        xs = _mix_call(xs, mods, l, hx, pa, pb, hd, o_x, o_c, mix_wts, ctx_len=ctx_len,
                       n_out_rows=rows, **dims)
        xs = _ffn_call(xs, None, mods, l, 2, 6, norm_g, f2w1, f2w3, f2w2,
                       final_g.reshape(1, d) if last else None, n_out_rows=rows, **dims)
    return xs.reshape(n_batch, seq, d)
```

```python
import functools
import math

import jax
import jax.numpy as jnp
import numpy as np
from jax import lax
from jax.experimental import pallas as pl
from jax.experimental.pallas import tpu as pltpu

F32 = jnp.float32
BF16 = jnp.bfloat16

EPS = 1e-6
GRID_W = 64
N_MOD = 9
ROPE_BASE = 10000.0

CONV_A_W = 256
CONV_A_K = 3
SGU_W = 256
SGU_GROUPS = 4
SGU_GROUP_W = SGU_W // SGU_GROUPS
CHUNK = 128
DA_HEADS = 4
DA_HEAD_DIM = 64
HEAD_W = 2 * DA_HEAD_DIM
DA_W = DA_HEADS * HEAD_W
ROPE_NF = DA_HEAD_DIM // 4
CONF_W = 256
CONF_K = 31
N_BRANCH = 4
Q_SCALE = DA_HEAD_DIM ** -0.5 * math.log2(math.e)

A_OFF = 0
B_OFF = A_OFF + 3 * CONV_A_W
Q_OFF = B_OFF + 2 * SGU_W
K_OFF = Q_OFF + DA_W
V_OFF = K_OFF + DA_W
D_OFF = V_OFF + DA_W
G_OFF = D_OFF + 2 * CONF_W

SUBLANES = 8
TM_TOK = 512
TM_FFN = 1024
FFN_RB = 256
TM_MIX = 512
TQ = 512
KB = 512
ATTN_HEADS_PER_STEP = 2
HALO = 16
CONV_RB = 128
MOD_ROWS = 8
VMEM_LIMIT = 56 * 1024 * 1024


def _cparams(n_grid, flags=None):
    return pltpu.CompilerParams(dimension_semantics=("arbitrary",) * n_grid,
                                vmem_limit_bytes=VMEM_LIMIT, flags=flags)


def _resident(shape):
    nd = len(shape)
    return pl.BlockSpec(shape, lambda *_: (0,) * nd, pipeline_mode=pl.Buffered(1))


def _layer_resident(arr, layer):
    nd = arr.ndim - 1
    return pl.BlockSpec((None,) + arr.shape[1:], lambda *_: (layer,) + (0,) * nd,
                        pipeline_mode=pl.Buffered(1))


def _dot(a, b):
    return jnp.dot(a, b, preferred_element_type=F32)


def _modnorm(x, g, shift, scale):
    y = x * lax.rsqrt(jnp.mean(x * x, axis=-1, keepdims=True) + EPS)
    return (y * g) * (1.0 + scale) + shift


def _ada_kernel(c_ref, w_ref, b_ref, o_ref):
    c = c_ref[...]
    s = (c * jax.nn.sigmoid(c)).astype(BF16)
    o_ref[...] = _dot(s, w_ref[...].astype(BF16)) + b_ref[...]


def _ada_call(cvec, w_ada, b_ada):
    depth, d, n = w_ada.shape
    tn = math.gcd(n, 1536)
    return pl.pallas_call(
        _ada_kernel,
        grid=(depth, n // tn),
        in_specs=[
            pl.BlockSpec((MOD_ROWS, d), lambda l, j: (0, 0)),
            pl.BlockSpec((None, d, tn), lambda l, j: (l, 0, j)),
            pl.BlockSpec((None, 1, tn), lambda l, j: (l, 0, j)),
        ],
        out_specs=pl.BlockSpec((None, MOD_ROWS, tn), lambda l, j: (l, 0, j)),
        out_shape=jax.ShapeDtypeStruct((depth, MOD_ROWS, n), F32),
        compiler_params=_cparams(2),
        name="ada_mod",
    )(cvec, w_ada, b_ada.reshape(depth, 1, n))


def _ffn_chunks(d_ff):
    tiles = d_ff // 256
    assert tiles * 256 == d_ff and tiles >= 2
    cut = (tiles + 1) // 2 * 256
    return ((0, cut), (cut, d_ff))


def _ffn_kernel(*refs, n_x_tiles, two_inputs, final, norm_row, mod_base, d_ff):
    if two_inputs:
        x_ref, c_ref, *refs = refs
    else:
        x_ref, *refs = refs
    if final:
        mod_ref, g_ref, w1_ref, w3_ref, w2_ref, fg_ref, o_ref = refs
    else:
        mod_ref, g_ref, w1_ref, w3_ref, w2_ref, o_ref = refs
    shift = mod_ref[mod_base:mod_base + 1, :]
    scale = mod_ref[mod_base + 1:mod_base + 2, :]
    gate = mod_ref[mod_base + 2:mod_base + 3, :]
    for r0 in range(0, x_ref.shape[0], FFN_RB):
        rs = slice(r0, r0 + FFN_RB)
        x = x_ref[rs, :]
        if two_inputs:
            x = jnp.where(pl.program_id(0) < n_x_tiles, x, c_ref[rs, :])
        h = _modnorm(x, g_ref[norm_row:norm_row + 1, :], shift, scale).astype(BF16)
        acc = None
        for lo, hi in _ffn_chunks(d_ff):
            a = _dot(h, w1_ref[:, lo:hi])
            b = _dot(h, w3_ref[:, lo:hi])
            gch = (a * jax.nn.sigmoid(a) * b).astype(BF16)
            part = _dot(gch, w2_ref[lo:hi, :])
            acc = part if acc is None else acc + part
        y = x + (0.5 * gate) * acc
        if final:
            y = y * lax.rsqrt(jnp.mean(y * y, axis=-1, keepdims=True) + EPS) * fg_ref[...]
        o_ref[rs, :] = y


def _mod_row(i, n_x_tiles, tiles_per_batch, n_batch):
    return jnp.where(i < n_x_tiles, i // tiles_per_batch, n_batch)


def _ffn_call(xs, ctx2d, mods, layer, norm_row, mod_base, norm_g, w1, w3, w2, final_g, *,
              n_batch, seq, n_out_rows):
    d = xs.shape[1]
    d_ff = w1.shape[2]
    two_inputs = ctx2d is not None
    tm = TM_TOK if two_inputs or n_out_rows % TM_FFN or seq % TM_FFN else TM_FFN
    n_x_tiles = n_batch * seq // tm
    n_tiles = n_out_rows // tm
    tiles_per_batch = seq // tm
    final = final_g is not None
    in_specs = []
    args = []
    if two_inputs:
        in_specs.append(pl.BlockSpec((tm, d), lambda i: (jnp.minimum(i, n_x_tiles - 1), 0)))
        in_specs.append(pl.BlockSpec((tm, d), lambda i: (jnp.maximum(i - n_x_tiles, 0), 0)))
        args += [xs, ctx2d]
    else:
        in_specs.append(pl.BlockSpec((tm, d), lambda i: (i, 0)))
        args.append(xs)
    in_specs += [
        pl.BlockSpec((None, None, N_MOD, d),
                     lambda i: (layer, _mod_row(i, n_x_tiles, tiles_per_batch, n_batch), 0, 0)),
        _layer_resident(norm_g, layer),
        _layer_resident(w1, layer),
        _layer_resident(w3, layer),
        _layer_resident(w2, layer),
    ]
    args += [mods, norm_g, w1, w3, w2]
    if final:
        in_specs.append(_resident((1, d)))
        args.append(final_g)
    kern = functools.partial(_ffn_kernel, n_x_tiles=n_x_tiles, two_inputs=two_inputs, final=final,
                             norm_row=norm_row, mod_base=mod_base, d_ff=d_ff)
    return pl.pallas_call(
        kern,
        grid=(n_tiles,),
        in_specs=in_specs,
        out_specs=pl.BlockSpec((tm, d), lambda i: (i, 0)),
        out_shape=jax.ShapeDtypeStruct((n_out_rows, d), F32),
        compiler_params=_cparams(1),
        name="ffn",
    )(*args)


def _gelu_tanh(x):
    c = math.sqrt(2.0 / math.pi)
    return x * (0.5 * (1.0 + jnp.tanh(c * (x + 0.044715 * (x * x * x)))))


def _proj_kernel(x_ref, mod_ref, g_ref, w_ref, cos_ref, sin_ref,
                 hx_ref, pa_ref, pb_ref, qT_ref, k_ref, vT_ref, hd_ref):
    x = x_ref[...]
    h = _modnorm(x, g_ref[1:2, :], mod_ref[3:4, :], mod_ref[4:5, :]).astype(BF16)
    hx_ref[...] = h

    def proj(lo, hi):
        return _dot(h, w_ref[:, lo:hi])

    pa = proj(A_OFF, B_OFF)
    pa_ref[:, 0:CONV_A_W] = pa[:, 0:CONV_A_W].astype(BF16)
    pa_ref[:, CONV_A_W:] = (pa[:, CONV_A_W:2 * CONV_A_W] * pa[:, 2 * CONV_A_W:]).astype(BF16)

    z = _gelu_tanh(proj(B_OFF, Q_OFF))
    v = z[:, SGU_W:]
    vc = v - jnp.mean(v, axis=-1, keepdims=True)
    vn = vc * lax.rsqrt(jnp.mean(vc * vc, axis=-1, keepdims=True) + EPS)
    pb_ref[:, 0:SGU_W] = z[:, 0:SGU_W].astype(BF16)
    pb_ref[:, SGU_W:] = vn.astype(BF16)

    cos = cos_ref[...]
    sin = sin_ref[...]
    lane = lax.broadcasted_iota(jnp.int32, cos.shape, 1)
    first_half = (lane % (2 * ROPE_NF)) < ROPE_NF

    def rope(blk):
        partner = jnp.where(first_half, pltpu.roll(blk, HEAD_W - ROPE_NF, 1), pltpu.roll(blk, ROPE_NF, 1))
        return blk * cos + partner * sin

    q = proj(Q_OFF, K_OFF)
    for hh in range(DA_HEADS):
        sl = slice(hh * HEAD_W, (hh + 1) * HEAD_W)
        qr = rope(q[:, sl]) * Q_SCALE
        qT_ref[sl, :] = qr.T.astype(BF16)
    k = proj(K_OFF, V_OFF)
    for hh in range(DA_HEADS):
        sl = slice(hh * HEAD_W, (hh + 1) * HEAD_W)
        k_ref[:, sl] = rope(k[:, sl]).astype(BF16)
    vv = proj(V_OFF, D_OFF)
    for hh in range(DA_HEADS):
        sl = slice(hh * HEAD_W, (hh + 1) * HEAD_W)
        vT_ref[sl, :] = vv[:, sl].T.astype(BF16)

    zd = proj(D_OFF, G_OFF)
    hd_ref[...] = (zd[:, 0:CONF_W] * jax.nn.sigmoid(zd[:, CONF_W:])).astype(BF16)


def _proj_call(xs, mods, layer, norm_g, w_in, cos_t, sin_t, *, n_batch, seq):
    ntok, d = xs.shape
    tm = TM_TOK
    n_x_tiles = n_batch * seq // tm
    tiles_per_batch = seq // tm
    row = lambda i: (i, 0)
    col = lambda i: (0, i)
    rope_row = lambda i: (jnp.where(i < n_x_tiles, i % tiles_per_batch, tiles_per_batch), 0)
    outs = pl.pallas_call(
        _proj_kernel,
        grid=(ntok // tm,),
        in_specs=[
            pl.BlockSpec((tm, d), row),
            pl.BlockSpec((None, None, N_MOD, d),
                         lambda i: (layer, _mod_row(i, n_x_tiles, tiles_per_batch, n_batch), 0, 0)),
            _layer_resident(norm_g, layer),
            pl.BlockSpec((None, d, G_OFF), lambda i: (layer, 0, 0), pipeline_mode=pl.Buffered(1)),
            pl.BlockSpec((tm, HEAD_W), rope_row),
            pl.BlockSpec((tm, HEAD_W), rope_row),
        ],
        out_specs=[
            pl.BlockSpec((tm, d), row),
            pl.BlockSpec((tm, 2 * CONV_A_W), row),
            pl.BlockSpec((tm, 2 * SGU_W), row),
            pl.BlockSpec((DA_W, tm), col),
            pl.BlockSpec((tm, DA_W), row),
            pl.BlockSpec((DA_W, tm), col),
            pl.BlockSpec((tm, CONF_W), row),
        ],
        out_shape=[
            jax.ShapeDtypeStruct((ntok, d), BF16),
            jax.ShapeDtypeStruct((ntok, 2 * CONV_A_W), BF16),
            jax.ShapeDtypeStruct((ntok, 2 * SGU_W), BF16),
            jax.ShapeDtypeStruct((DA_W, ntok), BF16),
            jax.ShapeDtypeStruct((ntok, DA_W), BF16),
            jax.ShapeDtypeStruct((DA_W, ntok), BF16),
            jax.ShapeDtypeStruct((ntok, CONF_W), BF16),
        ],
        compiler_params=_cparams(1),
        name="mixer_proj",
    )(xs, mods, norm_g, w_in, cos_t, sin_t)
    return outs


def _attn_kernel(*refs, has_latent, lam_init, tq):
    if has_latent:
        lam_ref, sg_ref, qT_ref, kl_ref, vl_ref, kc_ref, vc_ref, o_ref, s_scr = refs
    else:
        lam_ref, sg_ref, qT_ref, kc_ref, vc_ref, o_ref, s_scr = refs
    n_ctx = kc_ref.shape[0]
    blocks = []
    if has_latent:
        seq = kl_ref.shape[0]
        blocks += [(kl_ref, vl_ref, r0, KB, r0) for r0 in range(0, seq, KB)]
        blocks.append((kc_ref, vc_ref, 0, n_ctx, seq))
    else:
        blocks.append((kc_ref, vc_ref, 0, n_ctx, 0))

    lp = lam_ref[...]
    lam = (jnp.exp(jnp.sum(lp[0:1, :] * lp[1:2, :], axis=-1, keepdims=True))
           - jnp.exp(jnp.sum(lp[2:3, :] * lp[3:4, :], axis=-1, keepdims=True)) + lam_init)
    sub = lax.broadcasted_iota(jnp.int32, (HEAD_W, tq), 0)
    n_maps = 2 * (qT_ref.shape[0] // HEAD_W)

    def head_lanes(i):
        return slice((i // 2) * HEAD_W, (i // 2 + 1) * HEAD_W)

    def masked_q(i):
        qh = qT_ref[head_lanes(i), :]
        in_map = (sub < DA_HEAD_DIM) if i % 2 == 0 else (sub >= DA_HEAD_DIM)
        return jnp.where(in_map, qh, jnp.zeros_like(qh))

    def finish_head(hh, o0, o1):
        oh = o0 - lam * o1
        y = oh * lax.rsqrt(jnp.mean(oh * oh, axis=0, keepdims=True) + EPS)
        y = y * (sg_ref[...] * (1.0 - lam_init))
        o_ref[:, hh * HEAD_W:(hh + 1) * HEAD_W] = y.T.astype(BF16)

    mx_prev = None
    o_even = None
    for i in range(n_maps + 1):
        qm = masked_q(i) if i < n_maps else None
        mx_new = None
        lsum = acc = None
        for k_ref, v_ref, r0, rows, srow in blocks:
            if i < n_maps:
                s = _dot(k_ref[r0:r0 + rows, head_lanes(i)], qm)
                s_scr[i % 2, srow:srow + rows, :] = s
                bm = jnp.max(s, axis=0, keepdims=True)
                mx_new = bm if mx_new is None else jnp.maximum(mx_new, bm)
            if i > 0:
                p = jnp.exp2(s_scr[(i - 1) % 2, srow:srow + rows, :] - mx_prev)
                ps = jnp.sum(p, axis=0, keepdims=True)
                pv = _dot(v_ref[head_lanes(i - 1), r0:r0 + rows], p.astype(BF16))
                lsum = ps if lsum is None else lsum + ps
                acc = pv if acc is None else acc + pv
        if i > 0:
            o_map = acc / lsum
            if (i - 1) % 2 == 0:
                o_even = o_map
            else:
                finish_head((i - 1) // 2, o_even, o_map)
        mx_prev = mx_new


def _attn_call(lam_p, subln_col, layer, qT, k, vT, lam_init, *, n_batch, seq, ctx_len, latent):
    n_x = n_batch * seq
    hw = ATTN_HEADS_PER_STEP * HEAD_W
    ctx_blk0 = n_x // ctx_len
    if latent:
        tq = TQ
        n_q = seq // tq
        n_keys = seq + ctx_len
        q_map = lambda b, h, j: (h, b * n_q + j)
        o_map = lambda b, h, j: (b * n_q + j, h)
        out_rows = n_x
    else:
        tq = ctx_len
        n_q = 1
        n_keys = ctx_len
        q_map = lambda b, h, j: (h, ctx_blk0 + b)
        o_map = lambda b, h, j: (b, h)
        out_rows = n_batch * ctx_len
    in_specs = [
        _layer_resident(lam_p, layer),
        _layer_resident(subln_col, layer),
        pl.BlockSpec((hw, tq), q_map),
    ]
    args = [lam_p, subln_col, qT]
    if latent:
        in_specs += [pl.BlockSpec((seq, hw), lambda b, h, j: (b, h)),
                     pl.BlockSpec((hw, seq), lambda b, h, j: (h, b))]
        args += [k, vT]
    in_specs += [pl.BlockSpec((ctx_len, hw), lambda b, h, j: (ctx_blk0 + b, h)),
                 pl.BlockSpec((hw, ctx_len), lambda b, h, j: (h, ctx_blk0 + b))]
    args += [k, vT]
    kern = functools.partial(_attn_kernel, has_latent=latent, lam_init=lam_init, tq=tq)
    return pl.pallas_call(
        kern,
        grid=(n_batch, DA_HEADS // ATTN_HEADS_PER_STEP, n_q),
        in_specs=in_specs,
        out_specs=pl.BlockSpec((tq, hw), o_map),
        out_shape=jax.ShapeDtypeStruct((out_rows, DA_W), BF16),
        scratch_shapes=[pltpu.VMEM((2, n_keys, tq), F32)],
        compiler_params=_cparams(3),
        name="diff_attn_latent" if latent else "diff_attn_ctx",
    )(*args)


def _attn_pipe_kernel(lam_ref, sg_ref, qT_ref, kl_ref, kc_ref, vl_ref, vc_ref, o_ref, s_scr, mx_scr, *,
                      lam_init, tq):
    step = pl.program_id(0)
    seq = kl_ref.shape[0]
    n_ctx = kc_ref.shape[0]
    blocks = [(kl_ref, vl_ref, r0, KB, r0) for r0 in range(0, seq, KB)]
    blocks.append((kc_ref, vc_ref, 0, n_ctx, seq))

    @pl.when(step == 0)
    def _():
        s_scr[1] = jnp.zeros(s_scr.shape[1:], F32)
        mx_scr[1] = jnp.zeros(mx_scr.shape[1:], F32)

    lp = lam_ref[...]
    lam = (jnp.exp(jnp.sum(lp[0:1, :] * lp[1:2, :], axis=-1, keepdims=True))
           - jnp.exp(jnp.sum(lp[2:3, :] * lp[3:4, :], axis=-1, keepdims=True)) + lam_init)

    def body(cur, prv):
        sub = lax.broadcasted_iota(jnp.int32, (HEAD_W, tq), 0)
        qh = qT_ref[...]
        o_maps = []
        for m in range(2):
            in_map = (sub < DA_HEAD_DIM) if m == 0 else (sub >= DA_HEAD_DIM)
            qm = jnp.where(in_map, qh, jnp.zeros_like(qh))
            mx_prev = mx_scr[prv, m, 0:1, :]
            mx_new = None
            lsum = acc = None
            for k_ref, v_ref, r0, rows, srow in blocks:
                s = _dot(k_ref[r0:r0 + rows, :], qm)
                s_scr[cur, m, srow:srow + rows, :] = s
                bm = jnp.max(s, axis=0, keepdims=True)
                mx_new = bm if mx_new is None else jnp.maximum(mx_new, bm)
                p = jnp.exp2(s_scr[prv, m, srow:srow + rows, :] - mx_prev)
                ps = jnp.sum(p, axis=0, keepdims=True)
                pv = _dot(v_ref[:, r0:r0 + rows], p.astype(BF16))
                lsum = ps if lsum is None else lsum + ps
                acc = pv if acc is None else acc + pv
            mx_scr[cur, m] = jnp.broadcast_to(mx_new, (SUBLANES, tq))
            o_maps.append(acc / lsum)
        oh = o_maps[0] - lam * o_maps[1]
        y = oh * lax.rsqrt(jnp.mean(oh * oh, axis=0, keepdims=True) + EPS)
        y = y * (sg_ref[...] * (1.0 - lam_init))
        o_ref[...] = y.T.astype(BF16)

    for cur in range(2):
        pl.when(step % 2 == cur)(functools.partial(body, cur, 1 - cur))


def _attn_pipe_call(lam_p, subln_col, layer, qT, k, vT, lam_init, *, n_batch, seq, ctx_len):
    n_x = n_batch * seq
    ctx_blk0 = n_x // ctx_len
    tq = TQ
    n_q = seq // tq
    total = n_batch * DA_HEADS * n_q

    def unit(t):
        return t // (DA_HEADS * n_q), (t // n_q) % DA_HEADS, t % n_q

    def cur(s):
        return unit(jnp.minimum(s, total - 1))

    def prv(s):
        return unit(jnp.maximum(s - 1, 0))

    def q_map(s):
        b, h, j = cur(s)
        return h, b * n_q + j

    def o_map(s):
        b, h, j = prv(s)
        return b * n_q + j, h

    in_specs = [
        _layer_resident(lam_p, layer),
        _layer_resident(subln_col, layer),
        pl.BlockSpec((HEAD_W, tq), q_map),
        pl.BlockSpec((seq, HEAD_W), lambda s: (cur(s)[0], cur(s)[1])),
        pl.BlockSpec((ctx_len, HEAD_W), lambda s: (ctx_blk0 + cur(s)[0], cur(s)[1])),
        pl.BlockSpec((HEAD_W, seq), lambda s: (prv(s)[1], prv(s)[0])),
        pl.BlockSpec((HEAD_W, ctx_len), lambda s: (prv(s)[1], ctx_blk0 + prv(s)[0])),
    ]
    n_keys = seq + ctx_len
    kern = functools.partial(_attn_pipe_kernel, lam_init=lam_init, tq=tq)
    return pl.pallas_call(
        kern,
        grid=(total + 1,),
        in_specs=in_specs,
        out_specs=pl.BlockSpec((tq, HEAD_W), o_map),
        out_shape=jax.ShapeDtypeStruct((n_x, DA_W), BF16),
        scratch_shapes=[pltpu.VMEM((2, 2, n_keys, tq), F32),
                        pltpu.VMEM((2, 2, SUBLANES, tq), F32)],
        compiler_params=_cparams(1),
        name="diff_attn_latent",
    )(lam_p, subln_col, qT, k, k, vT, vT)


def _mix_kernel(x_ref, mod_ref, hx_ref, pa_ref, pap_ref, pan_ref, pb_ref, hd_ref, hdp_ref, hdn_ref, o_ref,
                wg_ref, caw_ref, wa_ref, sw_ref, sb_ref, wb_ref, wc_ref, dw_ref, db_ref, lg_ref, lb_ref,
                wd_ref, wo_ref, out_ref, cxe, hde, hsh, gsc, bin_scr, yd_scr, *, tiles_per_seq, tm):
    d = x_ref.shape[1]
    in_seq = pl.program_id(0) % tiles_per_seq
    prev_zero = in_seq == 0
    next_zero = in_seq == tiles_per_seq - 1
    hx = hx_ref[...]

    def fill(ext, main, prev, nxt):
        ext[HALO:HALO + tm, :] = main.astype(F32)
        ext[0:HALO, :] = jnp.where(prev_zero, 0.0, prev.astype(F32))
        ext[HALO + tm:2 * HALO + tm, :] = jnp.where(next_zero, 0.0, nxt.astype(F32))

    def dwconv(ext, shifted, w_ref, taps, r0, rows):
        acc = None
        for j in range(taps):
            a, r = divmod(HALO + r0 + j - taps // 2, SUBLANES)
            if r == 0 or shifted is None:
                win = ext[pl.ds(a * SUBLANES + r, rows), :]
            else:
                win = shifted[r - 1, pl.ds(a * SUBLANES, rows), :]
            term = w_ref[j:j + 1, :] * win
            acc = term if acc is None else acc + term
        return acc

    def conv_a_task(r):
        def run():
            if r == 0:
                fill(cxe, pa_ref[:, CONV_A_W:], pap_ref[:, CONV_A_W:], pan_ref[:, CONV_A_W:])
            rs = slice(r * CONV_RB, (r + 1) * CONV_RB)
            y = pa_ref[rs, 0:CONV_A_W].astype(F32) * dwconv(cxe, None, caw_ref, CONV_A_K, r * CONV_RB, CONV_RB)
            bin_scr[0, rs, :] = y.astype(BF16)
        return run

    def sgu_task(n):
        def run():
            lane = lax.broadcasted_iota(jnp.int32, (CHUNK, SGU_W), 1)
            rs = slice(n * CHUNK, (n + 1) * CHUNK)
            vn = pb_ref[rs, SGU_W:]
            vbd = jnp.concatenate(
                [jnp.where(lane // SGU_GROUP_W == gi, vn, jnp.zeros_like(vn)) for gi in range(SGU_GROUPS)],
                axis=0)
            s = _dot(sw_ref[...], vbd) + sb_ref[...]
            bin_scr[1, rs, :] = (pb_ref[rs, 0:SGU_W].astype(F32) * s).astype(BF16)
        return run

    def conf_fill_task():
        fill(hde, hd_ref[...], hdp_ref[...], hdn_ref[...])
        n_sh = hsh.shape[1]
        for r in range(1, SUBLANES):
            hsh[r - 1, :, :] = hde[pl.ds(r, n_sh), :]

    def conf_task(r):
        def run():
            hcv = dwconv(hde, hsh, dw_ref, CONF_K, r * CONV_RB, CONV_RB) + db_ref[...]
            mu = jnp.mean(hcv, axis=-1, keepdims=True)
            hc = hcv - mu
            hn = hc * lax.rsqrt(jnp.mean(hc * hc, axis=-1, keepdims=True) + EPS)
            hn = hn * lg_ref[...] + lb_ref[...]
            bin_scr[2, r * CONV_RB:(r + 1) * CONV_RB, :] = (hn * jax.nn.sigmoid(hn)).astype(BF16)
        return run

    def gate_task(j, half):
        def run():
            cols = slice(half * (d // 2), (half + 1) * (d // 2))
            wcols = slice(j * d + cols.start, j * d + cols.stop)
            gsc[j, :, cols] = jax.nn.sigmoid(_dot(hx, wg_ref[:, wcols]))
        return run

    n_rb = tm // CONV_RB
    n_pieces = 2 * N_BRANCH
    conf_fill_task()
    for r in range(n_rb):
        for p in range(r * n_pieces // n_rb, (r + 1) * n_pieces // n_rb):
            gate_task(p // 2, p % 2)()
        conf_task(r)()
        rs = slice(r * CONV_RB, (r + 1) * CONV_RB)
        yd_scr[rs, :] = _dot(bin_scr[2, rs, :], wd_ref[...])
    for r in range(n_rb):
        conv_a_task(r)()
    merged = gsc[0] * _dot(bin_scr[0], wa_ref[...])
    for n in range(tm // CHUNK):
        sgu_task(n)()
    merged = merged + gsc[1] * _dot(bin_scr[1], wb_ref[...])
    merged = merged + gsc[2] * _dot(o_ref[...], wc_ref[...])
    merged = merged + gsc[3] * yd_scr[...]
    y = _dot(merged.astype(BF16), wo_ref[...])
    out_ref[...] = x_ref[...] + mod_ref[5:6, :] * y


def _mix_call(xs, mods, layer, hx, pa, pb, hd, o, wts, *, n_batch, seq, ctx_len, context):
    ntok, d = xs.shape
    n_x = n_batch * seq
    if context:
        tm = ctx_len
        tiles_per_seq = 1
        n_tiles = n_batch
        blk0 = n_x // tm
        mod_map = lambda i: (layer, n_batch, 0, 0)
    else:
        tm = TM_MIX
        tiles_per_seq = seq // tm
        n_tiles = n_x // tm
        blk0 = 0
        mod_map = lambda i: (layer, i // tiles_per_seq, 0, 0)
    assert tm % CONV_RB == 0 and tm % CHUNK == 0 and seq % tm == 0 and n_x % tm == 0
    hpt = tm // HALO
    n_halo_blocks = ntok // HALO
    row = lambda i: (blk0 + i, 0)
    prev = lambda i: (jnp.maximum((blk0 + i) * hpt - 1, 0), 0)
    nxt = lambda i: (jnp.minimum((blk0 + i + 1) * hpt, n_halo_blocks - 1), 0)
    in_specs = [
        pl.BlockSpec((tm, d), row),
        pl.BlockSpec((None, None, N_MOD, d), mod_map),
        pl.BlockSpec((tm, d), row),
        pl.BlockSpec((tm, 2 * CONV_A_W), row),
        pl.BlockSpec((HALO, 2 * CONV_A_W), prev),
        pl.BlockSpec((HALO, 2 * CONV_A_W), nxt),
        pl.BlockSpec((tm, 2 * SGU_W), row),
        pl.BlockSpec((tm, CONF_W), row),
        pl.BlockSpec((HALO, CONF_W), prev),
        pl.BlockSpec((HALO, CONF_W), nxt),
        pl.BlockSpec((tm, DA_W), lambda i: (i, 0)),
    ]
    in_specs += [_layer_resident(w, layer) for w in wts]
    kern = functools.partial(_mix_kernel, tiles_per_seq=tiles_per_seq, tm=tm)
    ext_rows = tm + 2 * HALO
    return pl.pallas_call(
        kern,
        grid=(n_tiles,),
        in_specs=in_specs,
        out_specs=pl.BlockSpec((tm, d), lambda i: (i, 0)),
        out_shape=jax.ShapeDtypeStruct((n_tiles * tm, d), F32),
        scratch_shapes=[pltpu.VMEM((ext_rows, CONV_A_W), F32),
                        pltpu.VMEM((ext_rows, CONF_W), F32),
                        pltpu.VMEM((SUBLANES - 1, ext_rows - SUBLANES, CONF_W), F32),
                        pltpu.VMEM((N_BRANCH, tm, d), F32),
                        pltpu.VMEM((N_BRANCH - 1, tm, CONV_A_W), BF16),
                        pltpu.VMEM((tm, d), F32)],
        compiler_params=_cparams(1),
        name="mixer_merge_ctx" if context else "mixer_merge",
    )(xs, mods, hx, pa, pa, pa, pb, hd, hd, hd, o, *wts)


def _rope_tables(seq, pad_rows):
    n_rows = seq // GRID_W
    row = np.repeat(np.arange(n_rows, dtype=np.float32), GRID_W)
    col = np.tile(np.arange(GRID_W, dtype=np.float32), n_rows)
    inv = (np.float32(ROPE_BASE) ** (-np.arange(ROPE_NF, dtype=np.float32) / np.float32(ROPE_NF))).astype(np.float32)
    ar = (row[:, None] * inv).astype(np.float32)
    ac = (col[:, None] * inv).astype(np.float32)
    cos64 = np.concatenate([np.cos(ar), np.cos(ar), np.cos(ac), np.cos(ac)], axis=1)
    sin64 = np.concatenate([-np.sin(ar), np.sin(ar), -np.sin(ac), np.sin(ac)], axis=1)
    cos_t = np.concatenate([cos64, cos64], axis=1)
    sin_t = np.concatenate([sin64, sin64], axis=1)
    cos_t = np.concatenate([cos_t, np.ones((pad_rows, HEAD_W), np.float32)], axis=0)
    sin_t = np.concatenate([sin_t, np.zeros((pad_rows, HEAD_W), np.float32)], axis=0)
    return jnp.asarray(cos_t, F32), jnp.asarray(sin_t, F32)


def kernel(x, c, ctx, c_ctx, w_ada, b_ada, norm_g, ffn1_w1, ffn1_w3, ffn1_w2, ffn2_w1, ffn2_w3, ffn2_w2, w_in, conv_a_w, w_a_out, sgu_w, sgu_b, w_b_out, lam_p, subln_g, w_c_out, conf_dw, conf_db, conf_ln_g, conf_ln_b, w_d_out, w_o, final_g):
    n_batch, seq, d = x.shape
    ctx_len = ctx.shape[1]
    depth = w_ada.shape[0]
    n_x = n_batch * seq
    n_c = n_batch * ctx_len
    ntok = n_x + n_c
    assert n_batch + 1 <= MOD_ROWS and seq % TM_TOK == 0 and n_c % TM_TOK == 0 and seq % TQ == 0 and seq % TM_MIX == 0
    assert seq % GRID_W == 0 and w_in.shape[2] == G_OFF + N_BRANCH * d
    dims = dict(n_batch=n_batch, seq=seq)

    bf = lambda a: a.astype(BF16)
    f1w1, f1w3, f1w2 = bf(ffn1_w1), bf(ffn1_w3), bf(ffn1_w2)
    f2w1, f2w3, f2w2 = bf(ffn2_w1), bf(ffn2_w3), bf(ffn2_w2)
    w_in_b = bf(w_in)
    sw_cat = bf(jnp.transpose(sgu_w, (0, 2, 1, 3)).reshape(depth, CHUNK, SGU_GROUPS * CHUNK))
    sb_full = jnp.repeat(jnp.transpose(sgu_b, (0, 2, 1)), SGU_GROUP_W, axis=2)
    mix_wts = (w_in_b[:, :, G_OFF:], conv_a_w, bf(w_a_out), sw_cat, sb_full, bf(w_b_out), bf(w_c_out), conf_dw,
               conf_db.reshape(depth, 1, CONF_W), conf_ln_g.reshape(depth, 1, CONF_W),
               conf_ln_b.reshape(depth, 1, CONF_W), bf(w_d_out), bf(w_o))
    subln_col = subln_g.reshape(depth, HEAD_W, 1)

    cvec = jnp.zeros((MOD_ROWS, d), F32).at[:n_batch].set(c).at[n_batch].set(c_ctx)
    mods = _ada_call(cvec, w_ada, b_ada).reshape(depth, MOD_ROWS, N_MOD, d)
    cos_t, sin_t = _rope_tables(seq, TM_TOK)

    xs = None
    for l in range(depth):
        last = l == depth - 1
        lam_init = 0.8 - 0.6 * float(np.exp(-0.3 * l))
        if l == 0:
            xs = _ffn_call(x.reshape(n_x, d), ctx.reshape(n_c, d), mods, l, 0, 0, norm_g, f1w1, f1w3, f1w2,
                           None, n_out_rows=ntok, **dims)
        else:
            xs = _ffn_call(xs, None, mods, l, 0, 0, norm_g, f1w1, f1w3, f1w2, None, n_out_rows=ntok, **dims)
        hx, pa, pb, qT, k, vT, hd = _proj_call(xs, mods, l, norm_g, w_in_b, cos_t, sin_t, **dims)
        o_x = _attn_pipe_call(lam_p, subln_col, l, qT, k, vT, lam_init, ctx_len=ctx_len, **dims)
        xs_x = _mix_call(xs, mods, l, hx, pa, pb, hd, o_x, mix_wts, ctx_len=ctx_len, context=False, **dims)
        if last:
            xs = _ffn_call(xs_x, None, mods, l, 2, 6, norm_g, f2w1, f2w3, f2w2, final_g.reshape(1, d),
                           n_out_rows=n_x, **dims)
        else:
            o_c = _attn_call(lam_p, subln_col, l, qT, k, vT, lam_init, ctx_len=ctx_len, latent=False, **dims)
            xs_c = _mix_call(xs, mods, l, hx, pa, pb, hd, o_c, mix_wts, ctx_len=ctx_len, context=True, **dims)
            xs = _ffn_call(xs_x, xs_c, mods, l, 2, 6, norm_g, f2w1, f2w3, f2w2, None, n_out_rows=ntok, **dims)
    return xs.reshape(n_batch, seq, d)
```

```python
import functools
import math

import jax
import jax.numpy as jnp
import numpy as np
from jax import lax
from jax.experimental import pallas as pl
from jax.experimental.pallas import tpu as pltpu

F32 = jnp.float32
BF16 = jnp.bfloat16

EPS = 1e-6
GRID_W = 64
N_MOD = 9
ROPE_BASE = 10000.0

CONV_A_W = 256
CONV_A_K = 3
SGU_W = 256
SGU_GROUPS = 4
SGU_GROUP_W = SGU_W // SGU_GROUPS
CHUNK = 128
DA_HEADS = 4
DA_HEAD_DIM = 64
HEAD_W = 2 * DA_HEAD_DIM
DA_W = DA_HEADS * HEAD_W
ROPE_NF = DA_HEAD_DIM // 4
CONF_W = 256
CONF_K = 31
N_BRANCH = 4
Q_SCALE = DA_HEAD_DIM ** -0.5 * math.log2(math.e)

A_OFF = 0
B_OFF = A_OFF + 3 * CONV_A_W
Q_OFF = B_OFF + 2 * SGU_W
K_OFF = Q_OFF + DA_W
V_OFF = K_OFF + DA_W
D_OFF = V_OFF + DA_W
G_OFF = D_OFF + 2 * CONF_W

SUBLANES = 8
TM_TOK = 512
TM_FFN = 1024
FFN_RB = 256
PROJ_RB = 256
TM_MIX = 512
TQ = 512
KB = 512
ATTN_HEADS_PER_STEP = 2
HALO = 16
CONV_RB = 128
MOD_ROWS = 8
VMEM_LIMIT = 56 * 1024 * 1024


def _cparams(n_grid, flags=None):
    return pltpu.CompilerParams(dimension_semantics=("arbitrary",) * n_grid,
                                vmem_limit_bytes=VMEM_LIMIT, flags=flags)


def _resident(shape):
    nd = len(shape)
    return pl.BlockSpec(shape, lambda *_: (0,) * nd, pipeline_mode=pl.Buffered(1))


def _layer_resident(arr, layer):
    nd = arr.ndim - 1
    return pl.BlockSpec((None,) + arr.shape[1:], lambda *_: (layer,) + (0,) * nd,
                        pipeline_mode=pl.Buffered(1))


def _dot(a, b):
    return jnp.dot(a, b, preferred_element_type=F32)


def _modnorm(x, g, shift, scale):
    y = x * lax.rsqrt(jnp.mean(x * x, axis=-1, keepdims=True) + EPS)
    return (y * g) * (1.0 + scale) + shift


def _ada_kernel(c_ref, w_ref, b_ref, o_ref):
    c = c_ref[...]
    s = (c * jax.nn.sigmoid(c)).astype(BF16)
    o_ref[...] = _dot(s, w_ref[...].astype(BF16)) + b_ref[...]


def _ada_call(cvec, w_ada, b_ada):
    depth, d, n = w_ada.shape
    tn = math.gcd(n, 1536)
    return pl.pallas_call(
        _ada_kernel,
        grid=(depth, n // tn),
        in_specs=[
            pl.BlockSpec((MOD_ROWS, d), lambda l, j: (0, 0)),
            pl.BlockSpec((None, d, tn), lambda l, j: (l, 0, j)),
            pl.BlockSpec((None, 1, tn), lambda l, j: (l, 0, j)),
        ],
        out_specs=pl.BlockSpec((None, MOD_ROWS, tn), lambda l, j: (l, 0, j)),
        out_shape=jax.ShapeDtypeStruct((depth, MOD_ROWS, n), F32),
        compiler_params=_cparams(2),
        name="ada_mod",
    )(cvec, w_ada, b_ada.reshape(depth, 1, n))


def _ffn_chunks(d_ff):
    tiles = d_ff // 256
    assert tiles * 256 == d_ff and tiles >= 2
    cut = (tiles + 1) // 2 * 256
    return ((0, cut), (cut, d_ff))


def _ffn_kernel(*refs, n_x_tiles, two_inputs, final, norm_row, mod_base, d_ff):
    if two_inputs:
        x_ref, c_ref, *refs = refs
    else:
        x_ref, *refs = refs
    if final:
        mod_ref, g_ref, w1_ref, w3_ref, w2_ref, fg_ref, o_ref = refs
    else:
        mod_ref, g_ref, w1_ref, w3_ref, w2_ref, o_ref = refs
    shift = mod_ref[mod_base:mod_base + 1, :]
    scale = mod_ref[mod_base + 1:mod_base + 2, :]
    gate = mod_ref[mod_base + 2:mod_base + 3, :]
    for r0 in range(0, x_ref.shape[0], FFN_RB):
        rs = slice(r0, r0 + FFN_RB)
        x = x_ref[rs, :]
        if two_inputs:
            x = jnp.where(pl.program_id(0) < n_x_tiles, x, c_ref[rs, :])
        h = _modnorm(x, g_ref[norm_row:norm_row + 1, :], shift, scale).astype(BF16)
        acc = None
        for lo, hi in _ffn_chunks(d_ff):
            a = _dot(h, w1_ref[:, lo:hi])
            b = _dot(h, w3_ref[:, lo:hi])
            gch = (a * jax.nn.sigmoid(a) * b).astype(BF16)
            part = _dot(gch, w2_ref[lo:hi, :])
            acc = part if acc is None else acc + part
        y = x + (0.5 * gate) * acc
        if final:
            y = y * lax.rsqrt(jnp.mean(y * y, axis=-1, keepdims=True) + EPS) * fg_ref[...]
        o_ref[rs, :] = y


def _mod_row(i, n_x_tiles, tiles_per_batch, n_batch):
    return jnp.where(i < n_x_tiles, i // tiles_per_batch, n_batch)


def _ffn_call(xs, ctx2d, mods, layer, norm_row, mod_base, norm_g, w1, w3, w2, final_g, *,
              n_batch, seq, n_out_rows):
    d = xs.shape[1]
    d_ff = w1.shape[2]
    two_inputs = ctx2d is not None
    tm = TM_TOK if n_out_rows % TM_FFN or seq % TM_FFN or (two_inputs and ctx2d.shape[0] % TM_FFN) else TM_FFN
    n_x_tiles = n_batch * seq // tm
    n_tiles = n_out_rows // tm
    tiles_per_batch = seq // tm
    final = final_g is not None
    in_specs = []
    args = []
    if two_inputs:
        in_specs.append(pl.BlockSpec((tm, d), lambda i: (jnp.minimum(i, n_x_tiles - 1), 0)))
        in_specs.append(pl.BlockSpec((tm, d), lambda i: (jnp.maximum(i - n_x_tiles, 0), 0)))
        args += [xs, ctx2d]
    else:
        in_specs.append(pl.BlockSpec((tm, d), lambda i: (i, 0)))
        args.append(xs)
    in_specs += [
        pl.BlockSpec((None, None, N_MOD, d),
                     lambda i: (layer, _mod_row(i, n_x_tiles, tiles_per_batch, n_batch), 0, 0)),
        _layer_resident(norm_g, layer),
        _layer_resident(w1, layer),
        _layer_resident(w3, layer),
        _layer_resident(w2, layer),
    ]
    args += [mods, norm_g, w1, w3, w2]
    if final:
        in_specs.append(_resident((1, d)))
        args.append(final_g)
    kern = functools.partial(_ffn_kernel, n_x_tiles=n_x_tiles, two_inputs=two_inputs, final=final,
                             norm_row=norm_row, mod_base=mod_base, d_ff=d_ff)
    return pl.pallas_call(
        kern,
        grid=(n_tiles,),
        in_specs=in_specs,
        out_specs=pl.BlockSpec((tm, d), lambda i: (i, 0)),
        out_shape=jax.ShapeDtypeStruct((n_out_rows, d), F32),
        compiler_params=_cparams(1),
        name="ffn",
    )(*args)


def _gelu_tanh(x):
    c = math.sqrt(2.0 / math.pi)
    return x * (0.5 * (1.0 + jnp.tanh(c * (x + 0.044715 * (x * x * x)))))


def _proj_kernel(x_ref, mod_ref, g_ref, w_ref, cos_ref, sin_ref,
                 hx_ref, pa_ref, pb_ref, qT_ref, k_ref, vT_ref, hd_ref):
    lane = lax.broadcasted_iota(jnp.int32, (PROJ_RB, HEAD_W), 1)
    first_half = (lane % (2 * ROPE_NF)) < ROPE_NF
    for r0 in range(0, x_ref.shape[0], PROJ_RB):
        rs = slice(r0, r0 + PROJ_RB)
        h = _modnorm(x_ref[rs, :], g_ref[1:2, :], mod_ref[3:4, :], mod_ref[4:5, :]).astype(BF16)
        hx_ref[rs, :] = h

        def proj(lo, hi):
            return _dot(h, w_ref[:, lo:hi])

        pa = proj(A_OFF, B_OFF)
        pa_ref[rs, 0:CONV_A_W] = pa[:, 0:CONV_A_W].astype(BF16)
        pa_ref[rs, CONV_A_W:] = (pa[:, CONV_A_W:2 * CONV_A_W] * pa[:, 2 * CONV_A_W:]).astype(BF16)

        z = _gelu_tanh(proj(B_OFF, Q_OFF))
        v = z[:, SGU_W:]
        vc = v - jnp.mean(v, axis=-1, keepdims=True)
        vn = vc * lax.rsqrt(jnp.mean(vc * vc, axis=-1, keepdims=True) + EPS)
        pb_ref[rs, 0:SGU_W] = z[:, 0:SGU_W].astype(BF16)
        pb_ref[rs, SGU_W:] = vn.astype(BF16)

        cos = cos_ref[rs, :]
        sin = sin_ref[rs, :]

        def rope(blk):
            partner = jnp.where(first_half, pltpu.roll(blk, HEAD_W - ROPE_NF, 1), pltpu.roll(blk, ROPE_NF, 1))
            return blk * cos + partner * sin

        q = proj(Q_OFF, K_OFF)
        for hh in range(DA_HEADS):
            sl = slice(hh * HEAD_W, (hh + 1) * HEAD_W)
            qr = rope(q[:, sl]) * Q_SCALE
            qT_ref[sl, rs] = qr.T.astype(BF16)
        k = proj(K_OFF, V_OFF)
        for hh in range(DA_HEADS):
            sl = slice(hh * HEAD_W, (hh + 1) * HEAD_W)
            k_ref[rs, sl] = rope(k[:, sl]).astype(BF16)
        vv = proj(V_OFF, D_OFF)
        for hh in range(DA_HEADS):
            sl = slice(hh * HEAD_W, (hh + 1) * HEAD_W)
            vT_ref[sl, rs] = vv[:, sl].T.astype(BF16)

        zd = proj(D_OFF, G_OFF)
        hd_ref[rs, :] = (zd[:, 0:CONF_W] * jax.nn.sigmoid(zd[:, CONF_W:])).astype(BF16)


def _proj_call(xs, mods, layer, norm_g, w_in, cos_t, sin_t, *, n_batch, seq):
    ntok, d = xs.shape
    tm = TM_TOK
    n_x_tiles = n_batch * seq // tm
    tiles_per_batch = seq // tm
    row = lambda i: (i, 0)
    col = lambda i: (0, i)
    rope_row = lambda i: (jnp.where(i < n_x_tiles, i % tiles_per_batch, tiles_per_batch), 0)
    outs = pl.pallas_call(
        _proj_kernel,
        grid=(ntok // tm,),
        in_specs=[
            pl.BlockSpec((tm, d), row),
            pl.BlockSpec((None, None, N_MOD, d),
                         lambda i: (layer, _mod_row(i, n_x_tiles, tiles_per_batch, n_batch), 0, 0)),
            _layer_resident(norm_g, layer),
            pl.BlockSpec((None, d, G_OFF), lambda i: (layer, 0, 0), pipeline_mode=pl.Buffered(1)),
            pl.BlockSpec((tm, HEAD_W), rope_row),
            pl.BlockSpec((tm, HEAD_W), rope_row),
        ],
        out_specs=[
            pl.BlockSpec((tm, d), row),
            pl.BlockSpec((tm, 2 * CONV_A_W), row),
            pl.BlockSpec((tm, 2 * SGU_W), row),
            pl.BlockSpec((DA_W, tm), col),
            pl.BlockSpec((tm, DA_W), row),
            pl.BlockSpec((DA_W, tm), col),
            pl.BlockSpec((tm, CONF_W), row),
        ],
        out_shape=[
            jax.ShapeDtypeStruct((ntok, d), BF16),
            jax.ShapeDtypeStruct((ntok, 2 * CONV_A_W), BF16),
            jax.ShapeDtypeStruct((ntok, 2 * SGU_W), BF16),
            jax.ShapeDtypeStruct((DA_W, ntok), BF16),
            jax.ShapeDtypeStruct((ntok, DA_W), BF16),
            jax.ShapeDtypeStruct((DA_W, ntok), BF16),
            jax.ShapeDtypeStruct((ntok, CONF_W), BF16),
        ],
        compiler_params=_cparams(1),
        name="mixer_proj",
    )(xs, mods, norm_g, w_in, cos_t, sin_t)
    return outs


def _attn_kernel(*refs, has_latent, lam_init, tq):
    if has_latent:
        lam_ref, sg_ref, qT_ref, kl_ref, vl_ref, kc_ref, vc_ref, o_ref, s_scr = refs
    else:
        lam_ref, sg_ref, qT_ref, kc_ref, vc_ref, o_ref, s_scr = refs
    n_ctx = kc_ref.shape[0]
    blocks = []
    if has_latent:
        seq = kl_ref.shape[0]
        blocks += [(kl_ref, vl_ref, r0, KB, r0) for r0 in range(0, seq, KB)]
        blocks.append((kc_ref, vc_ref, 0, n_ctx, seq))
    else:
        blocks.append((kc_ref, vc_ref, 0, n_ctx, 0))

    lp = lam_ref[...]
    lam = (jnp.exp(jnp.sum(lp[0:1, :] * lp[1:2, :], axis=-1, keepdims=True))
           - jnp.exp(jnp.sum(lp[2:3, :] * lp[3:4, :], axis=-1, keepdims=True)) + lam_init)
    sub = lax.broadcasted_iota(jnp.int32, (HEAD_W, tq), 0)
    n_maps = 2 * (qT_ref.shape[0] // HEAD_W)

    def head_lanes(i):
        return slice((i // 2) * HEAD_W, (i // 2 + 1) * HEAD_W)

    def masked_q(i):
        qh = qT_ref[head_lanes(i), :]
        in_map = (sub < DA_HEAD_DIM) if i % 2 == 0 else (sub >= DA_HEAD_DIM)
        return jnp.where(in_map, qh, jnp.zeros_like(qh))

    def finish_head(hh, o0, o1):
        oh = o0 - lam * o1
        y = oh * lax.rsqrt(jnp.mean(oh * oh, axis=0, keepdims=True) + EPS)
        y = y * (sg_ref[...] * (1.0 - lam_init))
        o_ref[:, hh * HEAD_W:(hh + 1) * HEAD_W] = y.T.astype(BF16)

    mx_prev = None
    o_even = None
    for i in range(n_maps + 1):
        qm = masked_q(i) if i < n_maps else None
        mx_new = None
        lsum = acc = None
        for k_ref, v_ref, r0, rows, srow in blocks:
            if i < n_maps:
                s = _dot(k_ref[r0:r0 + rows, head_lanes(i)], qm)
                s_scr[i % 2, srow:srow + rows, :] = s
                bm = jnp.max(s, axis=0, keepdims=True)
                mx_new = bm if mx_new is None else jnp.maximum(mx_new, bm)
            if i > 0:
                p = jnp.exp2(s_scr[(i - 1) % 2, srow:srow + rows, :] - mx_prev)
                ps = jnp.sum(p, axis=0, keepdims=True)
                pv = _dot(v_ref[head_lanes(i - 1), r0:r0 + rows], p.astype(BF16))
                lsum = ps if lsum is None else lsum + ps
                acc = pv if acc is None else acc + pv
        if i > 0:
            o_map = acc / lsum
            if (i - 1) % 2 == 0:
                o_even = o_map
            else:
                finish_head((i - 1) // 2, o_even, o_map)
        mx_prev = mx_new


def _attn_call(lam_p, subln_col, layer, qT, k, vT, lam_init, *, n_batch, seq, ctx_len, latent):
    n_x = n_batch * seq
    hw = ATTN_HEADS_PER_STEP * HEAD_W
    ctx_blk0 = n_x // ctx_len
    if latent:
        tq = TQ
        n_q = seq // tq
        n_keys = seq + ctx_len
        q_map = lambda b, h, j: (h, b * n_q + j)
        o_map = lambda b, h, j: (b * n_q + j, h)
        out_rows = n_x
    else:
        tq = ctx_len
        n_q = 1
        n_keys = ctx_len
        q_map = lambda b, h, j: (h, ctx_blk0 + b)
        o_map = lambda b, h, j: (b, h)
        out_rows = n_batch * ctx_len
    in_specs = [
        _layer_resident(lam_p, layer),
        _layer_resident(subln_col, layer),
        pl.BlockSpec((hw, tq), q_map),
    ]
    args = [lam_p, subln_col, qT]
    if latent:
        in_specs += [pl.BlockSpec((seq, hw), lambda b, h, j: (b, h)),
                     pl.BlockSpec((hw, seq), lambda b, h, j: (h, b))]
        args += [k, vT]
    in_specs += [pl.BlockSpec((ctx_len, hw), lambda b, h, j: (ctx_blk0 + b, h)),
                 pl.BlockSpec((hw, ctx_len), lambda b, h, j: (h, ctx_blk0 + b))]
    args += [k, vT]
    kern = functools.partial(_attn_kernel, has_latent=latent, lam_init=lam_init, tq=tq)
    return pl.pallas_call(
        kern,
        grid=(n_batch, DA_HEADS // ATTN_HEADS_PER_STEP, n_q),
        in_specs=in_specs,
        out_specs=pl.BlockSpec((tq, hw), o_map),
        out_shape=jax.ShapeDtypeStruct((out_rows, DA_W), BF16),
        scratch_shapes=[pltpu.VMEM((2, n_keys, tq), F32)],
        compiler_params=_cparams(3),
        name="diff_attn_latent" if latent else "diff_attn_ctx",
    )(*args)


def _attn_pipe_kernel(lam_ref, sg_ref, qT_ref, kl_ref, kc_ref, vl_ref, vc_ref, o_ref, s_scr, mx_scr, *,
                      lam_init, tq):
    step = pl.program_id(0)
    seq = kl_ref.shape[0]
    n_ctx = kc_ref.shape[0]
    blocks = [(kl_ref, vl_ref, r0, KB, r0) for r0 in range(0, seq, KB)]
    blocks.append((kc_ref, vc_ref, 0, n_ctx, seq))

    @pl.when(step == 0)
    def _():
        s_scr[1] = jnp.zeros(s_scr.shape[1:], F32)
        mx_scr[1] = jnp.zeros(mx_scr.shape[1:], F32)

    lp = lam_ref[...]
    lam = (jnp.exp(jnp.sum(lp[0:1, :] * lp[1:2, :], axis=-1, keepdims=True))
           - jnp.exp(jnp.sum(lp[2:3, :] * lp[3:4, :], axis=-1, keepdims=True)) + lam_init)

    def body(cur, prv):
        sub = lax.broadcasted_iota(jnp.int32, (HEAD_W, tq), 0)
        qh = qT_ref[...]
        o_maps = []
        for m in range(2):
            in_map = (sub < DA_HEAD_DIM) if m == 0 else (sub >= DA_HEAD_DIM)
            qm = jnp.where(in_map, qh, jnp.zeros_like(qh))
            mx_prev = mx_scr[prv, m, 0:1, :]
            mx_new = None
            lsum = acc = None
            for k_ref, v_ref, r0, rows, srow in blocks:
                s = _dot(k_ref[r0:r0 + rows, :], qm)
                s_scr[cur, m, srow:srow + rows, :] = s
                bm = jnp.max(s, axis=0, keepdims=True)
                mx_new = bm if mx_new is None else jnp.maximum(mx_new, bm)
                p = jnp.exp2(s_scr[prv, m, srow:srow + rows, :] - mx_prev)
                ps = jnp.sum(p, axis=0, keepdims=True)
                pv = _dot(v_ref[:, r0:r0 + rows], p.astype(BF16))
                lsum = ps if lsum is None else lsum + ps
                acc = pv if acc is None else acc + pv
            mx_scr[cur, m] = jnp.broadcast_to(mx_new, (SUBLANES, tq))
            o_maps.append(acc / lsum)
        oh = o_maps[0] - lam * o_maps[1]
        y = oh * lax.rsqrt(jnp.mean(oh * oh, axis=0, keepdims=True) + EPS)
        y = y * (sg_ref[...] * (1.0 - lam_init))
        o_ref[...] = y.T.astype(BF16)

    for cur in range(2):
        pl.when(step % 2 == cur)(functools.partial(body, cur, 1 - cur))


def _attn_pipe_call(lam_p, subln_col, layer, qT, k, vT, lam_init, *, n_batch, seq, ctx_len):
    n_x = n_batch * seq
    ctx_blk0 = n_x // ctx_len
    tq = TQ
    n_q = seq // tq
    total = n_batch * DA_HEADS * n_q

    def unit(t):
        return t // (DA_HEADS * n_q), (t // n_q) % DA_HEADS, t % n_q

    def cur(s):
        return unit(jnp.minimum(s, total - 1))

    def prv(s):
        return unit(jnp.maximum(s - 1, 0))

    def q_map(s):
        b, h, j = cur(s)
        return h, b * n_q + j

    def o_map(s):
        b, h, j = prv(s)
        return b * n_q + j, h

    in_specs = [
        _layer_resident(lam_p, layer),
        _layer_resident(subln_col, layer),
        pl.BlockSpec((HEAD_W, tq), q_map),
        pl.BlockSpec((seq, HEAD_W), lambda s: (cur(s)[0], cur(s)[1])),
        pl.BlockSpec((ctx_len, HEAD_W), lambda s: (ctx_blk0 + cur(s)[0], cur(s)[1])),
        pl.BlockSpec((HEAD_W, seq), lambda s: (prv(s)[1], prv(s)[0])),
        pl.BlockSpec((HEAD_W, ctx_len), lambda s: (prv(s)[1], ctx_blk0 + prv(s)[0])),
    ]
    n_keys = seq + ctx_len
    kern = functools.partial(_attn_pipe_kernel, lam_init=lam_init, tq=tq)
    return pl.pallas_call(
        kern,
        grid=(total + 1,),
        in_specs=in_specs,
        out_specs=pl.BlockSpec((tq, HEAD_W), o_map),
        out_shape=jax.ShapeDtypeStruct((n_x, DA_W), BF16),
        scratch_shapes=[pltpu.VMEM((2, 2, n_keys, tq), F32),
                        pltpu.VMEM((2, 2, SUBLANES, tq), F32)],
        compiler_params=_cparams(1),
        name="diff_attn_latent",
    )(lam_p, subln_col, qT, k, k, vT, vT)


def _mix_kernel(x_ref, mod_ref, hx_ref, pa_ref, pap_ref, pan_ref, pb_ref, hd_ref, hdp_ref, hdn_ref, o_ref,
                wg_ref, caw_ref, wa_ref, sw_ref, sb_ref, wb_ref, wc_ref, dw_ref, db_ref, lg_ref, lb_ref,
                wd_ref, wo_ref, out_ref, cxe, hde, hsh, gsc, bin_scr, yd_scr, *, tiles_per_seq, tm):
    d = x_ref.shape[1]
    in_seq = pl.program_id(0) % tiles_per_seq
    prev_zero = in_seq == 0
    next_zero = in_seq == tiles_per_seq - 1
    hx = hx_ref[...]

    def fill(ext, main, prev, nxt):
        ext[HALO:HALO + tm, :] = main.astype(F32)
        ext[0:HALO, :] = jnp.where(prev_zero, 0.0, prev.astype(F32))
        ext[HALO + tm:2 * HALO + tm, :] = jnp.where(next_zero, 0.0, nxt.astype(F32))

    def dwconv(ext, shifted, w_ref, taps, r0, rows):
        acc = None
        for j in range(taps):
            a, r = divmod(HALO + r0 + j - taps // 2, SUBLANES)
            if r == 0 or shifted is None:
                win = ext[pl.ds(a * SUBLANES + r, rows), :]
            else:
                win = shifted[r - 1, pl.ds(a * SUBLANES, rows), :]
            term = w_ref[j:j + 1, :] * win
            acc = term if acc is None else acc + term
        return acc

    def conv_a_task(r):
        def run():
            if r == 0:
                fill(cxe, pa_ref[:, CONV_A_W:], pap_ref[:, CONV_A_W:], pan_ref[:, CONV_A_W:])
            rs = slice(r * CONV_RB, (r + 1) * CONV_RB)
            y = pa_ref[rs, 0:CONV_A_W].astype(F32) * dwconv(cxe, None, caw_ref, CONV_A_K, r * CONV_RB, CONV_RB)
            bin_scr[0, rs, :] = y.astype(BF16)
        return run

    def sgu_task(n):
        def run():
            lane = lax.broadcasted_iota(jnp.int32, (CHUNK, SGU_W), 1)
            rs = slice(n * CHUNK, (n + 1) * CHUNK)
            vn = pb_ref[rs, SGU_W:]
            vbd = jnp.concatenate(
                [jnp.where(lane // SGU_GROUP_W == gi, vn, jnp.zeros_like(vn)) for gi in range(SGU_GROUPS)],
                axis=0)
            s = _dot(sw_ref[...], vbd) + sb_ref[...]
            bin_scr[1, rs, :] = (pb_ref[rs, 0:SGU_W].astype(F32) * s).astype(BF16)
        return run

    def conf_fill_task():
        fill(hde, hd_ref[...], hdp_ref[...], hdn_ref[...])
        n_sh = hsh.shape[1]
        for r in range(1, SUBLANES):
            hsh[r - 1, :, :] = hde[pl.ds(r, n_sh), :]

    def conf_task(r):
        def run():
            hcv = dwconv(hde, hsh, dw_ref, CONF_K, r * CONV_RB, CONV_RB) + db_ref[...]
            mu = jnp.mean(hcv, axis=-1, keepdims=True)
            hc = hcv - mu
            hn = hc * lax.rsqrt(jnp.mean(hc * hc, axis=-1, keepdims=True) + EPS)
            hn = hn * lg_ref[...] + lb_ref[...]
            bin_scr[2, r * CONV_RB:(r + 1) * CONV_RB, :] = (hn * jax.nn.sigmoid(hn)).astype(BF16)
        return run

    def gate_task(j, half):
        def run():
            cols = slice(half * (d // 2), (half + 1) * (d // 2))
            wcols = slice(j * d + cols.start, j * d + cols.stop)
            gsc[j, :, cols] = jax.nn.sigmoid(_dot(hx, wg_ref[:, wcols]))
        return run

    n_rb = tm // CONV_RB
    n_pieces = 2 * N_BRANCH
    conf_fill_task()
    for r in range(n_rb):
        for p in range(r * n_pieces // n_rb, (r + 1) * n_pieces // n_rb):
            gate_task(p // 2, p % 2)()
        conf_task(r)()
        rs = slice(r * CONV_RB, (r + 1) * CONV_RB)
        yd_scr[rs, :] = _dot(bin_scr[2, rs, :], wd_ref[...])
    for r in range(n_rb):
        conv_a_task(r)()
    merged = gsc[0] * _dot(bin_scr[0], wa_ref[...])
    for n in range(tm // CHUNK):
        sgu_task(n)()
    merged = merged + gsc[1] * _dot(bin_scr[1], wb_ref[...])
    merged = merged + gsc[2] * _dot(o_ref[...], wc_ref[...])
    merged = merged + gsc[3] * yd_scr[...]
    y = _dot(merged.astype(BF16), wo_ref[...])
    out_ref[...] = x_ref[...] + mod_ref[5:6, :] * y


def _mix_call(xs, mods, layer, hx, pa, pb, hd, o, wts, *, n_batch, seq, ctx_len, context):
    ntok, d = xs.shape
    n_x = n_batch * seq
    if context:
        tm = ctx_len
        tiles_per_seq = 1
        n_tiles = n_batch
        blk0 = n_x // tm
        mod_map = lambda i: (layer, n_batch, 0, 0)
    else:
        tm = TM_MIX
        tiles_per_seq = seq // tm
        n_tiles = n_x // tm
        blk0 = 0
        mod_map = lambda i: (layer, i // tiles_per_seq, 0, 0)
    assert tm % CONV_RB == 0 and tm % CHUNK == 0 and seq % tm == 0 and n_x % tm == 0
    hpt = tm // HALO
    n_halo_blocks = ntok // HALO
    row = lambda i: (blk0 + i, 0)
    prev = lambda i: (jnp.maximum((blk0 + i) * hpt - 1, 0), 0)
    nxt = lambda i: (jnp.minimum((blk0 + i + 1) * hpt, n_halo_blocks - 1), 0)
    in_specs = [
        pl.BlockSpec((tm, d), row),
        pl.BlockSpec((None, None, N_MOD, d), mod_map),
        pl.BlockSpec((tm, d), row),
        pl.BlockSpec((tm, 2 * CONV_A_W), row),
        pl.BlockSpec((HALO, 2 * CONV_A_W), prev),
        pl.BlockSpec((HALO, 2 * CONV_A_W), nxt),
        pl.BlockSpec((tm, 2 * SGU_W), row),
        pl.BlockSpec((tm, CONF_W), row),
        pl.BlockSpec((HALO, CONF_W), prev),
        pl.BlockSpec((HALO, CONF_W), nxt),
        pl.BlockSpec((tm, DA_W), lambda i: (i, 0)),
    ]
    in_specs += [_layer_resident(w, layer) for w in wts]
    kern = functools.partial(_mix_kernel, tiles_per_seq=tiles_per_seq, tm=tm)
    ext_rows = tm + 2 * HALO
    return pl.pallas_call(
        kern,
        grid=(n_tiles,),
        in_specs=in_specs,
        out_specs=pl.BlockSpec((tm, d), lambda i: (i, 0)),
        out_shape=jax.ShapeDtypeStruct((n_tiles * tm, d), F32),
        scratch_shapes=[pltpu.VMEM((ext_rows, CONV_A_W), F32),
                        pltpu.VMEM((ext_rows, CONF_W), F32),
                        pltpu.VMEM((SUBLANES - 1, ext_rows - SUBLANES, CONF_W), F32),
                        pltpu.VMEM((N_BRANCH, tm, d), F32),
                        pltpu.VMEM((N_BRANCH - 1, tm, CONV_A_W), BF16),
                        pltpu.VMEM((tm, d), F32)],
        compiler_params=_cparams(1),
        name="mixer_merge_ctx" if context else "mixer_merge",
    )(xs, mods, hx, pa, pa, pa, pb, hd, hd, hd, o, *wts)


def _rope_tables(seq, pad_rows):
    n_rows = seq // GRID_W
    row = np.repeat(np.arange(n_rows, dtype=np.float32), GRID_W)
    col = np.tile(np.arange(GRID_W, dtype=np.float32), n_rows)
    inv = (np.float32(ROPE_BASE) ** (-np.arange(ROPE_NF, dtype=np.float32) / np.float32(ROPE_NF))).astype(np.float32)
    ar = (row[:, None] * inv).astype(np.float32)
    ac = (col[:, None] * inv).astype(np.float32)
    cos64 = np.concatenate([np.cos(ar), np.cos(ar), np.cos(ac), np.cos(ac)], axis=1)
    sin64 = np.concatenate([-np.sin(ar), np.sin(ar), -np.sin(ac), np.sin(ac)], axis=1)
    cos_t = np.concatenate([cos64, cos64], axis=1)
    sin_t = np.concatenate([sin64, sin64], axis=1)
    cos_t = np.concatenate([cos_t, np.ones((pad_rows, HEAD_W), np.float32)], axis=0)
    sin_t = np.concatenate([sin_t, np.zeros((pad_rows, HEAD_W), np.float32)], axis=0)
    return jnp.asarray(cos_t, F32), jnp.asarray(sin_t, F32)


def kernel(x, c, ctx, c_ctx, w_ada, b_ada, norm_g, ffn1_w1, ffn1_w3, ffn1_w2, ffn2_w1, ffn2_w3, ffn2_w2, w_in, conv_a_w, w_a_out, sgu_w, sgu_b, w_b_out, lam_p, subln_g, w_c_out, conf_dw, conf_db, conf_ln_g, conf_ln_b, w_d_out, w_o, final_g):
    n_batch, seq, d = x.shape
    ctx_len = ctx.shape[1]
    depth = w_ada.shape[0]
    n_x = n_batch * seq
    n_c = n_batch * ctx_len
    ntok = n_x + n_c
    assert n_batch + 1 <= MOD_ROWS and seq % TM_TOK == 0 and n_c % TM_TOK == 0 and seq % TQ == 0 and seq % TM_MIX == 0
    assert seq % GRID_W == 0 and w_in.shape[2] == G_OFF + N_BRANCH * d
    dims = dict(n_batch=n_batch, seq=seq)

    bf = lambda a: a.astype(BF16)
    f1w1, f1w3, f1w2 = bf(ffn1_w1), bf(ffn1_w3), bf(ffn1_w2)
    f2w1, f2w3, f2w2 = bf(ffn2_w1), bf(ffn2_w3), bf(ffn2_w2)
    w_in_b = bf(w_in)
    sw_cat = bf(jnp.transpose(sgu_w, (0, 2, 1, 3)).reshape(depth, CHUNK, SGU_GROUPS * CHUNK))
    sb_full = jnp.repeat(jnp.transpose(sgu_b, (0, 2, 1)), SGU_GROUP_W, axis=2)
    mix_wts = (w_in_b[:, :, G_OFF:], conv_a_w, bf(w_a_out), sw_cat, sb_full, bf(w_b_out), bf(w_c_out), conf_dw,
               conf_db.reshape(depth, 1, CONF_W), conf_ln_g.reshape(depth, 1, CONF_W),
               conf_ln_b.reshape(depth, 1, CONF_W), bf(w_d_out), bf(w_o))
    subln_col = subln_g.reshape(depth, HEAD_W, 1)

    cvec = jnp.zeros((MOD_ROWS, d), F32).at[:n_batch].set(c).at[n_batch].set(c_ctx)
    mods = _ada_call(cvec, w_ada, b_ada).reshape(depth, MOD_ROWS, N_MOD, d)
    cos_t, sin_t = _rope_tables(seq, TM_TOK)

    xs = None
    for l in range(depth):
        last = l == depth - 1
        lam_init = 0.8 - 0.6 * float(np.exp(-0.3 * l))
        if l == 0:
            xs = _ffn_call(x.reshape(n_x, d), ctx.reshape(n_c, d), mods, l, 0, 0, norm_g, f1w1, f1w3, f1w2,
                           None, n_out_rows=ntok, **dims)
        else:
            xs = _ffn_call(xs, None, mods, l, 0, 0, norm_g, f1w1, f1w3, f1w2, None, n_out_rows=ntok, **dims)
        hx, pa, pb, qT, k, vT, hd = _proj_call(xs, mods, l, norm_g, w_in_b, cos_t, sin_t, **dims)
        o_x = _attn_pipe_call(lam_p, subln_col, l, qT, k, vT, lam_init, ctx_len=ctx_len, **dims)
        xs_x = _mix_call(xs, mods, l, hx, pa, pb, hd, o_x, mix_wts, ctx_len=ctx_len, context=False, **dims)
        if last:
            xs = _ffn_call(xs_x, None, mods, l, 2, 6, norm_g, f2w1, f2w3, f2w2, final_g.reshape(1, d),
                           n_out_rows=n_x, **dims)
        else:
            o_c = _attn_call(lam_p, subln_col, l, qT, k, vT, lam_init, ctx_len=ctx_len, latent=False, **dims)
            xs_c = _mix_call(xs, mods, l, hx, pa, pb, hd, o_c, mix_wts, ctx_len=ctx_len, context=True, **dims)
            xs = _ffn_call(xs_x, xs_c, mods, l, 2, 6, norm_g, f2w1, f2w3, f2w2, None, n_out_rows=ntok, **dims)
    return xs.reshape(n_batch, seq, d)
```

```python
import functools
import math

import jax
import jax.numpy as jnp
import numpy as np
from jax import lax
from jax.experimental import pallas as pl
from jax.experimental.pallas import tpu as pltpu

F32 = jnp.float32
BF16 = jnp.bfloat16

EPS = 1e-6
GRID_W = 64
N_MOD = 9
ROPE_BASE = 10000.0

CONV_A_W = 256
CONV_A_K = 3
SGU_W = 256
SGU_GROUPS = 4
SGU_GROUP_W = SGU_W // SGU_GROUPS
CHUNK = 128
DA_HEADS = 4
DA_HEAD_DIM = 64
HEAD_W = 2 * DA_HEAD_DIM
DA_W = DA_HEADS * HEAD_W
ROPE_NF = DA_HEAD_DIM // 4
CONF_W = 256
CONF_K = 31
N_BRANCH = 4
Q_SCALE = DA_HEAD_DIM ** -0.5 * math.log2(math.e)

A_OFF = 0
B_OFF = A_OFF + 3 * CONV_A_W
Q_OFF = B_OFF + 2 * SGU_W
K_OFF = Q_OFF + DA_W
V_OFF = K_OFF + DA_W
D_OFF = V_OFF + DA_W
G_OFF = D_OFF + 2 * CONF_W

SUBLANES = 8
TM_TOK = 512
TM_FFN = 1024
FFN_RB = 256
PROJ_RB = 256
TM_MIX = 512
TQ_PIPE = 1024
KB_PIPE = 256
HALO = 16
CONV_RB = 128
MOD_ROWS = 8
VMEM_LIMIT = 56 * 1024 * 1024


def _cparams(n_grid, flags=None):
    return pltpu.CompilerParams(dimension_semantics=("arbitrary",) * n_grid,
                                vmem_limit_bytes=VMEM_LIMIT, flags=flags)


def _resident(shape):
    nd = len(shape)
    return pl.BlockSpec(shape, lambda *_: (0,) * nd, pipeline_mode=pl.Buffered(1))


def _layer_resident(arr, layer):
    nd = arr.ndim - 1
    return pl.BlockSpec((None,) + arr.shape[1:], lambda *_: (layer,) + (0,) * nd,
                        pipeline_mode=pl.Buffered(1))


def _dot(a, b):
    return jnp.dot(a, b, preferred_element_type=F32)


def _modnorm(x, g, shift, scale):
    y = x * lax.rsqrt(jnp.mean(x * x, axis=-1, keepdims=True) + EPS)
    return (y * g) * (1.0 + scale) + shift


def _ada_kernel(c_ref, w_ref, b_ref, o_ref):
    c = c_ref[...]
    s = (c * jax.nn.sigmoid(c)).astype(BF16)
    o_ref[...] = _dot(s, w_ref[...].astype(BF16)) + b_ref[...]


def _ada_call(cvec, w_ada, b_ada):
    depth, d, n = w_ada.shape
    tn = math.gcd(n, 1536)
    return pl.pallas_call(
        _ada_kernel,
        grid=(depth, n // tn),
        in_specs=[
            pl.BlockSpec((MOD_ROWS, d), lambda l, j: (0, 0)),
            pl.BlockSpec((None, d, tn), lambda l, j: (l, 0, j)),
            pl.BlockSpec((None, 1, tn), lambda l, j: (l, 0, j)),
        ],
        out_specs=pl.BlockSpec((None, MOD_ROWS, tn), lambda l, j: (l, 0, j)),
        out_shape=jax.ShapeDtypeStruct((depth, MOD_ROWS, n), F32),
        compiler_params=_cparams(2),
        name="ada_mod",
    )(cvec, w_ada, b_ada.reshape(depth, 1, n))


def _ffn_chunks(d_ff):
    tiles = d_ff // 256
    assert tiles * 256 == d_ff and tiles >= 2
    cut = (tiles + 1) // 2 * 256
    return ((0, cut), (cut, d_ff))


def _ffn_kernel(*refs, n_x_tiles, two_inputs, final, norm_row, mod_base, d_ff):
    if two_inputs:
        x_ref, c_ref, *refs = refs
    else:
        x_ref, *refs = refs
    if final:
        mod_ref, g_ref, w1_ref, w3_ref, w2_ref, fg_ref, o_ref = refs
    else:
        mod_ref, g_ref, w1_ref, w3_ref, w2_ref, o_ref = refs
    shift = mod_ref[mod_base:mod_base + 1, :]
    scale = mod_ref[mod_base + 1:mod_base + 2, :]
    gate = mod_ref[mod_base + 2:mod_base + 3, :]
    for r0 in range(0, x_ref.shape[0], FFN_RB):
        rs = slice(r0, r0 + FFN_RB)
        x = x_ref[rs, :]
        if two_inputs:
            x = jnp.where(pl.program_id(0) < n_x_tiles, x, c_ref[rs, :])
        h = _modnorm(x, g_ref[norm_row:norm_row + 1, :], shift, scale).astype(BF16)
        acc = None
        for lo, hi in _ffn_chunks(d_ff):
            a = _dot(h, w1_ref[:, lo:hi])
            b = _dot(h, w3_ref[:, lo:hi])
            gch = (a * jax.nn.sigmoid(a) * b).astype(BF16)
            part = _dot(gch, w2_ref[lo:hi, :])
            acc = part if acc is None else acc + part
        y = x + (0.5 * gate) * acc
        if final:
            y = y * lax.rsqrt(jnp.mean(y * y, axis=-1, keepdims=True) + EPS) * fg_ref[...]
        o_ref[rs, :] = y


def _mod_row(i, n_x_tiles, tiles_per_batch, n_batch):
    return jnp.where(i < n_x_tiles, i // tiles_per_batch, n_batch)


def _ffn_call(xs, ctx2d, mods, layer, norm_row, mod_base, norm_g, w1, w3, w2, final_g, *,
              n_batch, seq, n_out_rows):
    d = xs.shape[1]
    d_ff = w1.shape[2]
    two_inputs = ctx2d is not None
    tm = TM_TOK if n_out_rows % TM_FFN or seq % TM_FFN or (two_inputs and ctx2d.shape[0] % TM_FFN) else TM_FFN
    n_x_tiles = n_batch * seq // tm
    n_tiles = n_out_rows // tm
    tiles_per_batch = seq // tm
    final = final_g is not None
    in_specs = []
    args = []
    if two_inputs:
        in_specs.append(pl.BlockSpec((tm, d), lambda i: (jnp.minimum(i, n_x_tiles - 1), 0)))
        in_specs.append(pl.BlockSpec((tm, d), lambda i: (jnp.maximum(i - n_x_tiles, 0), 0)))
        args += [xs, ctx2d]
    else:
        in_specs.append(pl.BlockSpec((tm, d), lambda i: (i, 0)))
        args.append(xs)
    in_specs += [
        pl.BlockSpec((None, None, N_MOD, d),
                     lambda i: (layer, _mod_row(i, n_x_tiles, tiles_per_batch, n_batch), 0, 0)),
        _layer_resident(norm_g, layer),
        _layer_resident(w1, layer),
        _layer_resident(w3, layer),
        _layer_resident(w2, layer),
    ]
    args += [mods, norm_g, w1, w3, w2]
    if final:
        in_specs.append(_resident((1, d)))
        args.append(final_g)
    kern = functools.partial(_ffn_kernel, n_x_tiles=n_x_tiles, two_inputs=two_inputs, final=final,
                             norm_row=norm_row, mod_base=mod_base, d_ff=d_ff)
    return pl.pallas_call(
        kern,
        grid=(n_tiles,),
        in_specs=in_specs,
        out_specs=pl.BlockSpec((tm, d), lambda i: (i, 0)),
        out_shape=jax.ShapeDtypeStruct((n_out_rows, d), F32),
        compiler_params=_cparams(1),
        name="ffn",
    )(*args)


def _gelu_tanh(x):
    c = math.sqrt(2.0 / math.pi)
    return x * (0.5 * (1.0 + jnp.tanh(c * (x + 0.044715 * (x * x * x)))))


def _proj_kernel(x_ref, mod_ref, g_ref, w_ref, cos_ref, sin_ref,
                 hx_ref, pa_ref, pb_ref, qT_ref, k_ref, vT_ref, hd_ref):
    lane = lax.broadcasted_iota(jnp.int32, (PROJ_RB, HEAD_W), 1)
    first_half = (lane % (2 * ROPE_NF)) < ROPE_NF
    for r0 in range(0, x_ref.shape[0], PROJ_RB):
        rs = slice(r0, r0 + PROJ_RB)
        h = _modnorm(x_ref[rs, :], g_ref[1:2, :], mod_ref[3:4, :], mod_ref[4:5, :]).astype(BF16)
        hx_ref[rs, :] = h

        def proj(lo, hi):
            return _dot(h, w_ref[:, lo:hi])

        pa = proj(A_OFF, B_OFF)
        pa_ref[rs, 0:CONV_A_W] = pa[:, 0:CONV_A_W].astype(BF16)
        pa_ref[rs, CONV_A_W:] = (pa[:, CONV_A_W:2 * CONV_A_W] * pa[:, 2 * CONV_A_W:]).astype(BF16)

        z = _gelu_tanh(proj(B_OFF, Q_OFF))
        v = z[:, SGU_W:]
        vc = v - jnp.mean(v, axis=-1, keepdims=True)
        vn = vc * lax.rsqrt(jnp.mean(vc * vc, axis=-1, keepdims=True) + EPS)
        pb_ref[rs, 0:SGU_W] = z[:, 0:SGU_W].astype(BF16)
        pb_ref[rs, SGU_W:] = vn.astype(BF16)

        cos = cos_ref[rs, :]
        sin = sin_ref[rs, :]

        def rope(blk):
            partner = jnp.where(first_half, pltpu.roll(blk, HEAD_W - ROPE_NF, 1), pltpu.roll(blk, ROPE_NF, 1))
            return blk * cos + partner * sin

        q = proj(Q_OFF, K_OFF)
        for hh in range(DA_HEADS):
            sl = slice(hh * HEAD_W, (hh + 1) * HEAD_W)
            qr = rope(q[:, sl]) * Q_SCALE
            qT_ref[sl, rs] = qr.T.astype(BF16)
        k = proj(K_OFF, V_OFF)
        for hh in range(DA_HEADS):
            sl = slice(hh * HEAD_W, (hh + 1) * HEAD_W)
            k_ref[rs, sl] = rope(k[:, sl]).astype(BF16)
        vv = proj(V_OFF, D_OFF)
        for hh in range(DA_HEADS):
            sl = slice(hh * HEAD_W, (hh + 1) * HEAD_W)
            vT_ref[sl, rs] = vv[:, sl].T.astype(BF16)

        zd = proj(D_OFF, G_OFF)
        hd_ref[rs, :] = (zd[:, 0:CONF_W] * jax.nn.sigmoid(zd[:, CONF_W:])).astype(BF16)


def _proj_call(xs, mods, layer, norm_g, w_in, cos_t, sin_t, *, n_batch, seq):
    ntok, d = xs.shape
    tm = TM_TOK
    n_x_tiles = n_batch * seq // tm
    tiles_per_batch = seq // tm
    row = lambda i: (i, 0)
    col = lambda i: (0, i)
    rope_row = lambda i: (jnp.where(i < n_x_tiles, i % tiles_per_batch, tiles_per_batch), 0)
    outs = pl.pallas_call(
        _proj_kernel,
        grid=(ntok // tm,),
        in_specs=[
            pl.BlockSpec((tm, d), row),
            pl.BlockSpec((None, None, N_MOD, d),
                         lambda i: (layer, _mod_row(i, n_x_tiles, tiles_per_batch, n_batch), 0, 0)),
            _layer_resident(norm_g, layer),
            pl.BlockSpec((None, d, G_OFF), lambda i: (layer, 0, 0), pipeline_mode=pl.Buffered(1)),
            pl.BlockSpec((tm, HEAD_W), rope_row),
            pl.BlockSpec((tm, HEAD_W), rope_row),
        ],
        out_specs=[
            pl.BlockSpec((tm, d), row),
            pl.BlockSpec((tm, 2 * CONV_A_W), row),
            pl.BlockSpec((tm, 2 * SGU_W), row),
            pl.BlockSpec((DA_W, tm), col),
            pl.BlockSpec((tm, DA_W), row),
            pl.BlockSpec((DA_W, tm), col),
            pl.BlockSpec((tm, CONF_W), row),
        ],
        out_shape=[
            jax.ShapeDtypeStruct((ntok, d), BF16),
            jax.ShapeDtypeStruct((ntok, 2 * CONV_A_W), BF16),
            jax.ShapeDtypeStruct((ntok, 2 * SGU_W), BF16),
            jax.ShapeDtypeStruct((DA_W, ntok), BF16),
            jax.ShapeDtypeStruct((ntok, DA_W), BF16),
            jax.ShapeDtypeStruct((DA_W, ntok), BF16),
            jax.ShapeDtypeStruct((ntok, CONF_W), BF16),
        ],
        compiler_params=_cparams(1),
        name="mixer_proj",
    )(xs, mods, norm_g, w_in, cos_t, sin_t)
    return outs


def _lambda(lam_ref, lam_init):
    lp = lam_ref[...]
    return (jnp.exp(jnp.sum(lp[0:1, :] * lp[1:2, :], axis=-1, keepdims=True))
            - jnp.exp(jnp.sum(lp[2:3, :] * lp[3:4, :], axis=-1, keepdims=True)) + lam_init)


def _finish_head(o0, o1, lam, sg_ref, lam_init):
    oh = o0 - lam * o1
    y = oh * lax.rsqrt(jnp.mean(oh * oh, axis=0, keepdims=True) + EPS)
    return (y * (sg_ref[...] * (1.0 - lam_init))).T.astype(BF16)


def _attn_ctx_kernel(lam_ref, sg_ref, qT_ref, k_ref, vT_ref, o_ref, *, lam_init):
    tq = qT_ref.shape[1]
    lam = _lambda(lam_ref, lam_init)
    sub = lax.broadcasted_iota(jnp.int32, (HEAD_W, tq), 0)
    for hh in range(DA_HEADS):
        hs = slice(hh * HEAD_W, (hh + 1) * HEAD_W)
        qh = qT_ref[hs, :]
        o_maps = []
        for m in range(2):
            in_map = (sub < DA_HEAD_DIM) if m == 0 else (sub >= DA_HEAD_DIM)
            qm = jnp.where(in_map, qh, jnp.zeros_like(qh))
            s = _dot(k_ref[:, hs], qm)
            p = jnp.exp2(s - jnp.max(s, axis=0, keepdims=True))
            o_maps.append(_dot(vT_ref[hs, :], p.astype(BF16)) / jnp.sum(p, axis=0, keepdims=True))
        o_ref[:, hs] = _finish_head(o_maps[0], o_maps[1], lam, sg_ref, lam_init)


def _attn_ctx_call(lam_p, subln_col, layer, qT, k, vT, lam_init, *, n_batch, seq, ctx_len):
    blk0 = n_batch * seq // ctx_len
    kern = functools.partial(_attn_ctx_kernel, lam_init=lam_init)
    return pl.pallas_call(
        kern,
        grid=(n_batch,),
        in_specs=[
            _layer_resident(lam_p, layer),
            _layer_resident(subln_col, layer),
            pl.BlockSpec((DA_W, ctx_len), lambda b: (0, blk0 + b)),
            pl.BlockSpec((ctx_len, DA_W), lambda b: (blk0 + b, 0)),
            pl.BlockSpec((DA_W, ctx_len), lambda b: (0, blk0 + b)),
        ],
        out_specs=pl.BlockSpec((ctx_len, DA_W), lambda b: (b, 0)),
        out_shape=jax.ShapeDtypeStruct((n_batch * ctx_len, DA_W), BF16),
        compiler_params=_cparams(1),
        name="diff_attn_ctx",
    )(lam_p, subln_col, qT, k, vT)


def _attn_pipe_kernel(lam_ref, sg_ref, qT_ref, kl_ref, kc_ref, vlp_ref, vcp_ref, vlc_ref, vcc_ref, o_ref,
                      s_scr, mx_scr, o0_scr, *, lam_init, tq):
    step = pl.program_id(0)
    seq = kl_ref.shape[0]
    n_ctx = kc_ref.shape[0]
    blocks = [(r0, KB_PIPE, r0, False) for r0 in range(0, seq, KB_PIPE)] + [(0, n_ctx, seq, True)]

    @pl.when(step == 0)
    def _():
        s_scr[1] = jnp.zeros(s_scr.shape[1:], F32)
        mx_scr[...] = jnp.zeros(mx_scr.shape, F32)
        o0_scr[...] = jnp.zeros(o0_scr.shape, F32)

    lam = _lambda(lam_ref, lam_init)
    sub = lax.broadcasted_iota(jnp.int32, (HEAD_W, tq), 0)
    qh = qT_ref[...]
    zero = jnp.zeros_like(qh)

    def half(qm, w_slot, r_slot, mx_r, vl_ref, vc_ref):
        mx_w = None
        lsum = acc = None
        for r0, rows, srow, is_ctx in blocks:
            k_ref, v_ref = (kc_ref, vc_ref) if is_ctx else (kl_ref, vl_ref)
            s = _dot(k_ref[r0:r0 + rows, :], qm)
            s_scr[w_slot, srow:srow + rows, :] = s
            bm = jnp.max(s, axis=0, keepdims=True)
            mx_w = bm if mx_w is None else jnp.maximum(mx_w, bm)
            p = jnp.exp2(s_scr[r_slot, srow:srow + rows, :] - mx_r)
            ps = jnp.sum(p, axis=0, keepdims=True)
            pv = _dot(v_ref[:, r0:r0 + rows], p.astype(BF16))
            lsum = ps if lsum is None else lsum + ps
            acc = pv if acc is None else acc + pv
        return mx_w, acc / lsum

    mx0, o1_prev = half(jnp.where(sub < DA_HEAD_DIM, qh, zero), 0, 1, mx_scr[0:1, :], vlp_ref, vcp_ref)
    o_ref[...] = _finish_head(o0_scr[...], o1_prev, lam, sg_ref, lam_init)
    mx1, o0 = half(jnp.where(sub >= DA_HEAD_DIM, qh, zero), 1, 0, mx0, vlc_ref, vcc_ref)
    mx_scr[...] = jnp.broadcast_to(mx1, mx_scr.shape)
    o0_scr[...] = o0


def _attn_pipe_call(lam_p, subln_col, layer, qT, k, vT, lam_init, *, n_batch, seq, ctx_len):
    n_x = n_batch * seq
    ctx_blk0 = n_x // ctx_len
    tq = TQ_PIPE
    n_q = seq // tq
    total = n_batch * DA_HEADS * n_q

    def unit(t):
        return t // (DA_HEADS * n_q), (t // n_q) % DA_HEADS, t % n_q

    def cur(s):
        return unit(jnp.minimum(s, total - 1))

    def prv(s):
        return unit(jnp.maximum(s - 1, 0))

    def q_map(s):
        b, h, j = cur(s)
        return h, b * n_q + j

    def o_map(s):
        b, h, j = prv(s)
        return b * n_q + j, h

    in_specs = [
        _layer_resident(lam_p, layer),
        _layer_resident(subln_col, layer),
        pl.BlockSpec((HEAD_W, tq), q_map),
        pl.BlockSpec((seq, HEAD_W), lambda s: (cur(s)[0], cur(s)[1])),
        pl.BlockSpec((ctx_len, HEAD_W), lambda s: (ctx_blk0 + cur(s)[0], cur(s)[1])),
        pl.BlockSpec((HEAD_W, seq), lambda s: (prv(s)[1], prv(s)[0])),
        pl.BlockSpec((HEAD_W, ctx_len), lambda s: (prv(s)[1], ctx_blk0 + prv(s)[0])),
        pl.BlockSpec((HEAD_W, seq), lambda s: (cur(s)[1], cur(s)[0])),
        pl.BlockSpec((HEAD_W, ctx_len), lambda s: (cur(s)[1], ctx_blk0 + cur(s)[0])),
    ]
    n_keys = seq + ctx_len
    kern = functools.partial(_attn_pipe_kernel, lam_init=lam_init, tq=tq)
    return pl.pallas_call(
        kern,
        grid=(total + 1,),
        in_specs=in_specs,
        out_specs=pl.BlockSpec((tq, HEAD_W), o_map),
        out_shape=jax.ShapeDtypeStruct((n_x, DA_W), BF16),
        scratch_shapes=[pltpu.VMEM((2, n_keys, tq), F32),
                        pltpu.VMEM((SUBLANES, tq), F32),
                        pltpu.VMEM((HEAD_W, tq), F32)],
        compiler_params=_cparams(1),
        name="diff_attn_latent",
    )(lam_p, subln_col, qT, k, k, vT, vT, vT, vT)


def _mix_kernel(x_ref, mod_ref, hx_ref, pa_ref, pap_ref, pan_ref, pb_ref, hd_ref, hdp_ref, hdn_ref, o_ref,
                wg_ref, caw_ref, wa_ref, sw_ref, sb_ref, wb_ref, wc_ref, dw_ref, db_ref, lg_ref, lb_ref,
                wd_ref, wo_ref, out_ref, cxe, hde, hsh, gsc, bin_scr, yd_scr, *, tiles_per_seq, tm):
    d = x_ref.shape[1]
    in_seq = pl.program_id(0) % tiles_per_seq
    prev_zero = in_seq == 0
    next_zero = in_seq == tiles_per_seq - 1
    hx = hx_ref[...]

    def fill(ext, main, prev, nxt):
        ext[HALO:HALO + tm, :] = main.astype(F32)
        ext[0:HALO, :] = jnp.where(prev_zero, 0.0, prev.astype(F32))
        ext[HALO + tm:2 * HALO + tm, :] = jnp.where(next_zero, 0.0, nxt.astype(F32))

    def dwconv(ext, shifted, w_ref, taps, r0, rows):
        acc = None
        for j in range(taps):
            a, r = divmod(HALO + r0 + j - taps // 2, SUBLANES)
            if r == 0 or shifted is None:
                win = ext[pl.ds(a * SUBLANES + r, rows), :]
            else:
                win = shifted[r - 1, pl.ds(a * SUBLANES, rows), :]
            term = w_ref[j:j + 1, :] * win
            acc = term if acc is None else acc + term
        return acc

    def conv_a_task(r):
        def run():
            if r == 0:
                fill(cxe, pa_ref[:, CONV_A_W:], pap_ref[:, CONV_A_W:], pan_ref[:, CONV_A_W:])
            rs = slice(r * CONV_RB, (r + 1) * CONV_RB)
            y = pa_ref[rs, 0:CONV_A_W].astype(F32) * dwconv(cxe, None, caw_ref, CONV_A_K, r * CONV_RB, CONV_RB)
            bin_scr[0, rs, :] = y.astype(BF16)
        return run

    def sgu_task(n):
        def run():
            lane = lax.broadcasted_iota(jnp.int32, (CHUNK, SGU_W), 1)
            rs = slice(n * CHUNK, (n + 1) * CHUNK)
            vn = pb_ref[rs, SGU_W:]
            vbd = jnp.concatenate(
                [jnp.where(lane // SGU_GROUP_W == gi, vn, jnp.zeros_like(vn)) for gi in range(SGU_GROUPS)],
                axis=0)
            s = _dot(sw_ref[...], vbd) + sb_ref[...]
            bin_scr[1, rs, :] = (pb_ref[rs, 0:SGU_W].astype(F32) * s).astype(BF16)
        return run

    def conf_fill_task():
        fill(hde, hd_ref[...], hdp_ref[...], hdn_ref[...])
        n_sh = hsh.shape[1]
        for r in range(1, SUBLANES):
            hsh[r - 1, :, :] = hde[pl.ds(r, n_sh), :]

    def conf_task(r):
        def run():
            hcv = dwconv(hde, hsh, dw_ref, CONF_K, r * CONV_RB, CONV_RB) + db_ref[...]
            mu = jnp.mean(hcv, axis=-1, keepdims=True)
            hc = hcv - mu
            hn = hc * lax.rsqrt(jnp.mean(hc * hc, axis=-1, keepdims=True) + EPS)
            hn = hn * lg_ref[...] + lb_ref[...]
            bin_scr[2, r * CONV_RB:(r + 1) * CONV_RB, :] = (hn * jax.nn.sigmoid(hn)).astype(BF16)
        return run

    def gate_task(j, half):
        def run():
            cols = slice(half * (d // 2), (half + 1) * (d // 2))
            wcols = slice(j * d + cols.start, j * d + cols.stop)
            gsc[j, :, cols] = jax.nn.sigmoid(_dot(hx, wg_ref[:, wcols]))
        return run

    n_rb = tm // CONV_RB
    n_pieces = 2 * N_BRANCH
    conf_fill_task()
    for r in range(n_rb):
        for p in range(r * n_pieces // n_rb, (r + 1) * n_pieces // n_rb):
            gate_task(p // 2, p % 2)()
        conf_task(r)()
        rs = slice(r * CONV_RB, (r + 1) * CONV_RB)
        yd_scr[rs, :] = _dot(bin_scr[2, rs, :], wd_ref[...])
    for r in range(n_rb):
        conv_a_task(r)()
    merged = gsc[0] * _dot(bin_scr[0], wa_ref[...])
    for n in range(tm // CHUNK):
        sgu_task(n)()
    merged = merged + gsc[1] * _dot(bin_scr[1], wb_ref[...])
    merged = merged + gsc[2] * _dot(o_ref[...], wc_ref[...])
    merged = merged + gsc[3] * yd_scr[...]
    y = _dot(merged.astype(BF16), wo_ref[...])
    out_ref[...] = x_ref[...] + mod_ref[5:6, :] * y


def _mix_call(xs, mods, layer, hx, pa, pb, hd, o, wts, *, n_batch, seq, ctx_len, context):
    ntok, d = xs.shape
    n_x = n_batch * seq
    if context:
        tm = ctx_len
        tiles_per_seq = 1
        n_tiles = n_batch
        blk0 = n_x // tm
        mod_map = lambda i: (layer, n_batch, 0, 0)
    else:
        tm = TM_MIX
        tiles_per_seq = seq // tm
        n_tiles = n_x // tm
        blk0 = 0
        mod_map = lambda i: (layer, i // tiles_per_seq, 0, 0)
    assert tm % CONV_RB == 0 and tm % CHUNK == 0 and seq % tm == 0 and n_x % tm == 0
    hpt = tm // HALO
    n_halo_blocks = ntok // HALO
    row = lambda i: (blk0 + i, 0)
    prev = lambda i: (jnp.maximum((blk0 + i) * hpt - 1, 0), 0)
    nxt = lambda i: (jnp.minimum((blk0 + i + 1) * hpt, n_halo_blocks - 1), 0)
    in_specs = [
        pl.BlockSpec((tm, d), row),
        pl.BlockSpec((None, None, N_MOD, d), mod_map),
        pl.BlockSpec((tm, d), row),
        pl.BlockSpec((tm, 2 * CONV_A_W), row),
        pl.BlockSpec((HALO, 2 * CONV_A_W), prev),
        pl.BlockSpec((HALO, 2 * CONV_A_W), nxt),
        pl.BlockSpec((tm, 2 * SGU_W), row),
        pl.BlockSpec((tm, CONF_W), row),
        pl.BlockSpec((HALO, CONF_W), prev),
        pl.BlockSpec((HALO, CONF_W), nxt),
        pl.BlockSpec((tm, DA_W), lambda i: (i, 0)),
    ]
    in_specs += [_layer_resident(w, layer) for w in wts]
    kern = functools.partial(_mix_kernel, tiles_per_seq=tiles_per_seq, tm=tm)
    ext_rows = tm + 2 * HALO
    return pl.pallas_call(
        kern,
        grid=(n_tiles,),
        in_specs=in_specs,
        out_specs=pl.BlockSpec((tm, d), lambda i: (i, 0)),
        out_shape=jax.ShapeDtypeStruct((n_tiles * tm, d), F32),
        scratch_shapes=[pltpu.VMEM((ext_rows, CONV_A_W), F32),
                        pltpu.VMEM((ext_rows, CONF_W), F32),
                        pltpu.VMEM((SUBLANES - 1, ext_rows - SUBLANES, CONF_W), F32),
                        pltpu.VMEM((N_BRANCH, tm, d), F32),
                        pltpu.VMEM((N_BRANCH - 1, tm, CONV_A_W), BF16),
                        pltpu.VMEM((tm, d), F32)],
        compiler_params=_cparams(1),
        name="mixer_merge_ctx" if context else "mixer_merge",
    )(xs, mods, hx, pa, pa, pa, pb, hd, hd, hd, o, *wts)


def _rope_tables(seq, pad_rows):
    n_rows = seq // GRID_W
    row = np.repeat(np.arange(n_rows, dtype=np.float32), GRID_W)
    col = np.tile(np.arange(GRID_W, dtype=np.float32), n_rows)
    inv = (np.float32(ROPE_BASE) ** (-np.arange(ROPE_NF, dtype=np.float32) / np.float32(ROPE_NF))).astype(np.float32)
    ar = (row[:, None] * inv).astype(np.float32)
    ac = (col[:, None] * inv).astype(np.float32)
    cos64 = np.concatenate([np.cos(ar), np.cos(ar), np.cos(ac), np.cos(ac)], axis=1)
    sin64 = np.concatenate([-np.sin(ar), np.sin(ar), -np.sin(ac), np.sin(ac)], axis=1)
    cos_t = np.concatenate([cos64, cos64], axis=1)
    sin_t = np.concatenate([sin64, sin64], axis=1)
    cos_t = np.concatenate([cos_t, np.ones((pad_rows, HEAD_W), np.float32)], axis=0)
    sin_t = np.concatenate([sin_t, np.zeros((pad_rows, HEAD_W), np.float32)], axis=0)
    return jnp.asarray(cos_t, F32), jnp.asarray(sin_t, F32)


def kernel(x, c, ctx, c_ctx, w_ada, b_ada, norm_g, ffn1_w1, ffn1_w3, ffn1_w2, ffn2_w1, ffn2_w3, ffn2_w2, w_in, conv_a_w, w_a_out, sgu_w, sgu_b, w_b_out, lam_p, subln_g, w_c_out, conf_dw, conf_db, conf_ln_g, conf_ln_b, w_d_out, w_o, final_g):
    n_batch, seq, d = x.shape
    ctx_len = ctx.shape[1]
    depth = w_ada.shape[0]
    n_x = n_batch * seq
    n_c = n_batch * ctx_len
    ntok = n_x + n_c
    assert n_batch + 1 <= MOD_ROWS and seq % TM_TOK == 0 and n_c % TM_TOK == 0 and seq % TQ_PIPE == 0 and seq % TM_MIX == 0
    assert seq % GRID_W == 0 and w_in.shape[2] == G_OFF + N_BRANCH * d
    dims = dict(n_batch=n_batch, seq=seq)

    bf = lambda a: a.astype(BF16)
    f1w1, f1w3, f1w2 = bf(ffn1_w1), bf(ffn1_w3), bf(ffn1_w2)
    f2w1, f2w3, f2w2 = bf(ffn2_w1), bf(ffn2_w3), bf(ffn2_w2)
    w_in_b = bf(w_in[:, :, :G_OFF])
    sw_cat = bf(jnp.transpose(sgu_w, (0, 2, 1, 3)).reshape(depth, CHUNK, SGU_GROUPS * CHUNK))
    sb_full = jnp.repeat(jnp.transpose(sgu_b, (0, 2, 1)), SGU_GROUP_W, axis=2)
    mix_wts = (bf(w_in[:, :, G_OFF:]), conv_a_w, bf(w_a_out), sw_cat, sb_full, bf(w_b_out), bf(w_c_out), conf_dw,
               conf_db.reshape(depth, 1, CONF_W), conf_ln_g.reshape(depth, 1, CONF_W),
               conf_ln_b.reshape(depth, 1, CONF_W), bf(w_d_out), bf(w_o))
    subln_col = subln_g.reshape(depth, HEAD_W, 1)

    cvec = jnp.zeros((MOD_ROWS, d), F32).at[:n_batch].set(c).at[n_batch].set(c_ctx)
    mods = _ada_call(cvec, w_ada, b_ada).reshape(depth, MOD_ROWS, N_MOD, d)
    cos_t, sin_t = _rope_tables(seq, TM_TOK)

    xs = None
    for l in range(depth):
        last = l == depth - 1
        lam_init = 0.8 - 0.6 * float(np.exp(-0.3 * l))
        if l == 0:
            xs = _ffn_call(x.reshape(n_x, d), ctx.reshape(n_c, d), mods, l, 0, 0, norm_g, f1w1, f1w3, f1w2,
                           None, n_out_rows=ntok, **dims)
        else:
            xs = _ffn_call(xs, None, mods, l, 0, 0, norm_g, f1w1, f1w3, f1w2, None, n_out_rows=ntok, **dims)
        hx, pa, pb, qT, k, vT, hd = _proj_call(xs, mods, l, norm_g, w_in_b, cos_t, sin_t, **dims)
        o_x = _attn_pipe_call(lam_p, subln_col, l, qT, k, vT, lam_init, ctx_len=ctx_len, **dims)
        xs_x = _mix_call(xs, mods, l, hx, pa, pb, hd, o_x, mix_wts, ctx_len=ctx_len, context=False, **dims)
        if last:
            xs = _ffn_call(xs_x, None, mods, l, 2, 6, norm_g, f2w1, f2w3, f2w2, final_g.reshape(1, d),
                           n_out_rows=n_x, **dims)
        else:
            o_c = _attn_ctx_call(lam_p, subln_col, l, qT, k, vT, lam_init, ctx_len=ctx_len, **dims)
            xs_c = _mix_call(xs, mods, l, hx, pa, pb, hd, o_c, mix_wts, ctx_len=ctx_len, context=True, **dims)
            xs = _ffn_call(xs_x, xs_c, mods, l, 2, 6, norm_g, f2w1, f2w3, f2w2, None, n_out_rows=ntok, **dims)
    return xs.reshape(n_batch, seq, d)
```

```python
import functools
import math

import jax
import jax.numpy as jnp
import numpy as np
from jax import lax
from jax.experimental import pallas as pl
from jax.experimental.pallas import tpu as pltpu

F32 = jnp.float32
BF16 = jnp.bfloat16

EPS = 1e-6
GRID_W = 64
N_MOD = 9
ROPE_BASE = 10000.0

CONV_A_W = 256
CONV_A_K = 3
SGU_W = 256
SGU_GROUPS = 4
SGU_GROUP_W = SGU_W // SGU_GROUPS
CHUNK = 128
DA_HEADS = 4
DA_HEAD_DIM = 64
HEAD_W = 2 * DA_HEAD_DIM
DA_W = DA_HEADS * HEAD_W
ROPE_NF = DA_HEAD_DIM // 4
CONF_W = 256
CONF_K = 31
N_BRANCH = 4
Q_SCALE = DA_HEAD_DIM ** -0.5 * math.log2(math.e)

A_OFF = 0
B_OFF = A_OFF + 3 * CONV_A_W
Q_OFF = B_OFF + 2 * SGU_W
K_OFF = Q_OFF + DA_W
V_OFF = K_OFF + DA_W
D_OFF = V_OFF + DA_W
G_OFF = D_OFF + 2 * CONF_W

SUBLANES = 8
TM_TOK = 512
TM_FFN = 1024
FFN_RB = 256
PROJ_RB = 256
TM_MIX = 512
TQ = 512
KB = 512
HALO = 16
CONV_RB = 128
MOD_ROWS = 8
VMEM_LIMIT = 56 * 1024 * 1024


def _cparams(n_grid):
    return pltpu.CompilerParams(dimension_semantics=("arbitrary",) * n_grid,
                                vmem_limit_bytes=VMEM_LIMIT)


def _resident(shape):
    nd = len(shape)
    return pl.BlockSpec(shape, lambda *_: (0,) * nd, pipeline_mode=pl.Buffered(1))


def _layer_resident(arr, layer):
    nd = arr.ndim - 1
    return pl.BlockSpec((None,) + arr.shape[1:], lambda *_: (layer,) + (0,) * nd,
                        pipeline_mode=pl.Buffered(1))


def _dot(a, b):
    return jnp.dot(a, b, preferred_element_type=F32)


def _modnorm(x, g, shift, scale):
    y = x * lax.rsqrt(jnp.mean(x * x, axis=-1, keepdims=True) + EPS)
    return (y * g) * (1.0 + scale) + shift


def _ada_kernel(c_ref, w_ref, b_ref, o_ref):
    c = c_ref[...]
    s = (c * jax.nn.sigmoid(c)).astype(BF16)
    o_ref[...] = _dot(s, w_ref[...].astype(BF16)) + b_ref[...]


def _ada_call(cvec, w_ada, b_ada):
    depth, d, n = w_ada.shape
    tn = math.gcd(n, 1536)
    return pl.pallas_call(
        _ada_kernel,
        grid=(depth, n // tn),
        in_specs=[
            pl.BlockSpec((MOD_ROWS, d), lambda l, j: (0, 0)),
            pl.BlockSpec((None, d, tn), lambda l, j: (l, 0, j)),
            pl.BlockSpec((None, 1, tn), lambda l, j: (l, 0, j)),
        ],
        out_specs=pl.BlockSpec((None, MOD_ROWS, tn), lambda l, j: (l, 0, j)),
        out_shape=jax.ShapeDtypeStruct((depth, MOD_ROWS, n), F32),
        compiler_params=_cparams(2),
        name="ada_mod",
    )(cvec, w_ada, b_ada.reshape(depth, 1, n))


def _ffn_chunks(d_ff):
    tiles = d_ff // 256
    assert tiles * 256 == d_ff and tiles >= 2
    cut = (tiles + 1) // 2 * 256
    return ((0, cut), (cut, d_ff))


def _ffn_kernel(*refs, n_x_tiles, two_inputs, final, norm_row, mod_base, d_ff):
    if two_inputs:
        x_ref, c_ref, *refs = refs
    else:
        x_ref, *refs = refs
    if final:
        mod_ref, g_ref, w1_ref, w3_ref, w2_ref, fg_ref, o_ref = refs
    else:
        mod_ref, g_ref, w1_ref, w3_ref, w2_ref, o_ref = refs
    shift = mod_ref[mod_base:mod_base + 1, :]
    scale = mod_ref[mod_base + 1:mod_base + 2, :]
    gate = mod_ref[mod_base + 2:mod_base + 3, :]
    for r0 in range(0, x_ref.shape[0], FFN_RB):
        rs = slice(r0, r0 + FFN_RB)
        x = x_ref[rs, :]
        if two_inputs:
            x = jnp.where(pl.program_id(0) < n_x_tiles, x, c_ref[rs, :])
        h = _modnorm(x, g_ref[norm_row:norm_row + 1, :], shift, scale).astype(BF16)
        acc = None
        for lo, hi in _ffn_chunks(d_ff):
            a = _dot(h, w1_ref[:, lo:hi])
            b = _dot(h, w3_ref[:, lo:hi])
            gch = (a * jax.nn.sigmoid(a) * b).astype(BF16)
            part = _dot(gch, w2_ref[lo:hi, :])
            acc = part if acc is None else acc + part
        y = x + (0.5 * gate) * acc
        if final:
            y = y * lax.rsqrt(jnp.mean(y * y, axis=-1, keepdims=True) + EPS) * fg_ref[...]
        o_ref[rs, :] = y


def _mod_row(i, n_x_tiles, tiles_per_batch, n_batch):
    return jnp.where(i < n_x_tiles, i // tiles_per_batch, n_batch)


def _ffn_call(xs, ctx2d, mods, layer, norm_row, mod_base, norm_g, w1, w3, w2, final_g, *,
              n_batch, seq, n_out_rows):
    d = xs.shape[1]
    d_ff = w1.shape[2]
    two_inputs = ctx2d is not None
    tm = TM_TOK if n_out_rows % TM_FFN or seq % TM_FFN or (two_inputs and ctx2d.shape[0] % TM_FFN) else TM_FFN
    n_x_tiles = n_batch * seq // tm
    n_tiles = n_out_rows // tm
    tiles_per_batch = seq // tm
    final = final_g is not None
    in_specs = []
    args = []
    if two_inputs:
        in_specs.append(pl.BlockSpec((tm, d), lambda i: (jnp.minimum(i, n_x_tiles - 1), 0)))
        in_specs.append(pl.BlockSpec((tm, d), lambda i: (jnp.maximum(i - n_x_tiles, 0), 0)))
        args += [xs, ctx2d]
    else:
        in_specs.append(pl.BlockSpec((tm, d), lambda i: (i, 0)))
        args.append(xs)
    in_specs += [
        pl.BlockSpec((None, None, N_MOD, d),
                     lambda i: (layer, _mod_row(i, n_x_tiles, tiles_per_batch, n_batch), 0, 0)),
        _layer_resident(norm_g, layer),
        _layer_resident(w1, layer),
        _layer_resident(w3, layer),
        _layer_resident(w2, layer),
    ]
    args += [mods, norm_g, w1, w3, w2]
    if final:
        in_specs.append(_resident((1, d)))
        args.append(final_g)
    kern = functools.partial(_ffn_kernel, n_x_tiles=n_x_tiles, two_inputs=two_inputs, final=final,
                             norm_row=norm_row, mod_base=mod_base, d_ff=d_ff)
    return pl.pallas_call(
        kern,
        grid=(n_tiles,),
        in_specs=in_specs,
        out_specs=pl.BlockSpec((tm, d), lambda i: (i, 0)),
        out_shape=jax.ShapeDtypeStruct((n_out_rows, d), F32),
        compiler_params=_cparams(1),
        name="ffn",
    )(*args)


def _gelu_tanh(x):
    c = math.sqrt(2.0 / math.pi)
    return x * (0.5 * (1.0 + jnp.tanh(c * (x + 0.044715 * (x * x * x)))))


def _proj_kernel(x_ref, mod_ref, g_ref, w_ref, cos_ref, sin_ref,
                 hx_ref, pa_ref, pb_ref, qT_ref, k_ref, vT_ref, hd_ref):
    lane = lax.broadcasted_iota(jnp.int32, (PROJ_RB, HEAD_W), 1)
    first_half = (lane % (2 * ROPE_NF)) < ROPE_NF
    for r0 in range(0, x_ref.shape[0], PROJ_RB):
        rs = slice(r0, r0 + PROJ_RB)
        h = _modnorm(x_ref[rs, :], g_ref[1:2, :], mod_ref[3:4, :], mod_ref[4:5, :]).astype(BF16)
        hx_ref[rs, :] = h

        def proj(lo, hi):
            return _dot(h, w_ref[:, lo:hi])

        pa = proj(A_OFF, B_OFF)
        pa_ref[rs, 0:CONV_A_W] = pa[:, 0:CONV_A_W].astype(BF16)
        pa_ref[rs, CONV_A_W:] = (pa[:, CONV_A_W:2 * CONV_A_W] * pa[:, 2 * CONV_A_W:]).astype(BF16)

        z = _gelu_tanh(proj(B_OFF, Q_OFF))
        v = z[:, SGU_W:]
        vc = v - jnp.mean(v, axis=-1, keepdims=True)
        vn = vc * lax.rsqrt(jnp.mean(vc * vc, axis=-1, keepdims=True) + EPS)
        pb_ref[rs, 0:SGU_W] = z[:, 0:SGU_W].astype(BF16)
        pb_ref[rs, SGU_W:] = vn.astype(BF16)

        cos = cos_ref[rs, :]
        sin = sin_ref[rs, :]

        def rope(blk):
            partner = jnp.where(first_half, pltpu.roll(blk, HEAD_W - ROPE_NF, 1), pltpu.roll(blk, ROPE_NF, 1))
            return blk * cos + partner * sin

        q = proj(Q_OFF, K_OFF)
        for hh in range(DA_HEADS):
            sl = slice(hh * HEAD_W, (hh + 1) * HEAD_W)
            qr = rope(q[:, sl]) * Q_SCALE
            qT_ref[sl, rs] = qr.T.astype(BF16)
        k = proj(K_OFF, V_OFF)
        for hh in range(DA_HEADS):
            sl = slice(hh * HEAD_W, (hh + 1) * HEAD_W)
            k_ref[rs, sl] = rope(k[:, sl]).astype(BF16)
        vv = proj(V_OFF, D_OFF)
        for hh in range(DA_HEADS):
            sl = slice(hh * HEAD_W, (hh + 1) * HEAD_W)
            vT_ref[sl, rs] = vv[:, sl].T.astype(BF16)

        zd = proj(D_OFF, G_OFF)
        hd_ref[rs, :] = (zd[:, 0:CONF_W] * jax.nn.sigmoid(zd[:, CONF_W:])).astype(BF16)


def _proj_call(xs, mods, layer, norm_g, w_in, cos_t, sin_t, *, n_batch, seq):
    ntok, d = xs.shape
    tm = TM_TOK
    n_x_tiles = n_batch * seq // tm
    tiles_per_batch = seq // tm
    row = lambda i: (i, 0)
    col = lambda i: (0, i)
    rope_row = lambda i: (jnp.where(i < n_x_tiles, i % tiles_per_batch, tiles_per_batch), 0)
    outs = pl.pallas_call(
        _proj_kernel,
        grid=(ntok // tm,),
        in_specs=[
            pl.BlockSpec((tm, d), row),
            pl.BlockSpec((None, None, N_MOD, d),
                         lambda i: (layer, _mod_row(i, n_x_tiles, tiles_per_batch, n_batch), 0, 0)),
            _layer_resident(norm_g, layer),
            pl.BlockSpec((None, d, G_OFF), lambda i: (layer, 0, 0), pipeline_mode=pl.Buffered(1)),
            pl.BlockSpec((tm, HEAD_W), rope_row),
            pl.BlockSpec((tm, HEAD_W), rope_row),
        ],
        out_specs=[
            pl.BlockSpec((tm, d), row),
            pl.BlockSpec((tm, 2 * CONV_A_W), row),
            pl.BlockSpec((tm, 2 * SGU_W), row),
            pl.BlockSpec((DA_W, tm), col),
            pl.BlockSpec((tm, DA_W), row),
            pl.BlockSpec((DA_W, tm), col),
            pl.BlockSpec((tm, CONF_W), row),
        ],
        out_shape=[
            jax.ShapeDtypeStruct((ntok, d), BF16),
            jax.ShapeDtypeStruct((ntok, 2 * CONV_A_W), BF16),
            jax.ShapeDtypeStruct((ntok, 2 * SGU_W), BF16),
            jax.ShapeDtypeStruct((DA_W, ntok), BF16),
            jax.ShapeDtypeStruct((ntok, DA_W), BF16),
            jax.ShapeDtypeStruct((DA_W, ntok), BF16),
            jax.ShapeDtypeStruct((ntok, CONF_W), BF16),
        ],
        compiler_params=_cparams(1),
        name="mixer_proj",
    )(xs, mods, norm_g, w_in, cos_t, sin_t)
    return outs


def _lambda(lam_ref, lam_init):
    lp = lam_ref[...]
    return (jnp.exp(jnp.sum(lp[0:1, :] * lp[1:2, :], axis=-1, keepdims=True))
            - jnp.exp(jnp.sum(lp[2:3, :] * lp[3:4, :], axis=-1, keepdims=True)) + lam_init)


def _finish_head(o0, o1, lam, sg_ref, lam_init):
    oh = o0 - lam * o1
    y = oh * lax.rsqrt(jnp.mean(oh * oh, axis=0, keepdims=True) + EPS)
    return (y * (sg_ref[...] * (1.0 - lam_init))).T.astype(BF16)


def _attn_ctx_kernel(lam_ref, sg_ref, qT_ref, k_ref, vT_ref, o_ref, *, lam_init):
    tq = qT_ref.shape[1]
    lam = _lambda(lam_ref, lam_init)
    sub = lax.broadcasted_iota(jnp.int32, (HEAD_W, tq), 0)
    for hh in range(DA_HEADS):
        hs = slice(hh * HEAD_W, (hh + 1) * HEAD_W)
        qh = qT_ref[hs, :]
        o_maps = []
        for m in range(2):
            in_map = (sub < DA_HEAD_DIM) if m == 0 else (sub >= DA_HEAD_DIM)
            qm = jnp.where(in_map, qh, jnp.zeros_like(qh))
            s = _dot(k_ref[:, hs], qm)
            p = jnp.exp2(s - jnp.max(s, axis=0, keepdims=True))
            o_maps.append(_dot(vT_ref[hs, :], p.astype(BF16)) / jnp.sum(p, axis=0, keepdims=True))
        o_ref[:, hs] = _finish_head(o_maps[0], o_maps[1], lam, sg_ref, lam_init)


def _attn_ctx_call(lam_p, subln_col, layer, qT, k, vT, lam_init, *, n_batch, seq, ctx_len):
    blk0 = n_batch * seq // ctx_len
    kern = functools.partial(_attn_ctx_kernel, lam_init=lam_init)
    return pl.pallas_call(
        kern,
        grid=(n_batch,),
        in_specs=[
            _layer_resident(lam_p, layer),
            _layer_resident(subln_col, layer),
            pl.BlockSpec((DA_W, ctx_len), lambda b: (0, blk0 + b)),
            pl.BlockSpec((ctx_len, DA_W), lambda b: (blk0 + b, 0)),
            pl.BlockSpec((DA_W, ctx_len), lambda b: (0, blk0 + b)),
        ],
        out_specs=pl.BlockSpec((ctx_len, DA_W), lambda b: (b, 0)),
        out_shape=jax.ShapeDtypeStruct((n_batch * ctx_len, DA_W), BF16),
        compiler_params=_cparams(1),
        name="diff_attn_ctx",
    )(lam_p, subln_col, qT, k, vT)


def _attn_pipe_kernel(lam_ref, sg_ref, qT_ref, kl_ref, kc_ref, vl_ref, vc_ref, o_ref, s_scr, mx_scr, *,
                      lam_init, tq):
    step = pl.program_id(0)
    seq = kl_ref.shape[0]
    n_ctx = kc_ref.shape[0]
    blocks = [(kl_ref, vl_ref, r0, KB, r0) for r0 in range(0, seq, KB)]
    blocks.append((kc_ref, vc_ref, 0, n_ctx, seq))

    @pl.when(step == 0)
    def _():
        s_scr[1] = jnp.zeros(s_scr.shape[1:], F32)
        mx_scr[1] = jnp.zeros(mx_scr.shape[1:], F32)

    lam = _lambda(lam_ref, lam_init)

    def body(cur, prv):
        sub = lax.broadcasted_iota(jnp.int32, (HEAD_W, tq), 0)
        qh = qT_ref[...]
        o_maps = []
        for m in range(2):
            in_map = (sub < DA_HEAD_DIM) if m == 0 else (sub >= DA_HEAD_DIM)
            qm = jnp.where(in_map, qh, jnp.zeros_like(qh))
            mx_prev = mx_scr[prv, m, 0:1, :]
            mx_new = None
            lsum = acc = None
            for k_ref, v_ref, r0, rows, srow in blocks:
                s = _dot(k_ref[r0:r0 + rows, :], qm)
                s_scr[cur, m, srow:srow + rows, :] = s
                bm = jnp.max(s, axis=0, keepdims=True)
                mx_new = bm if mx_new is None else jnp.maximum(mx_new, bm)
                p = jnp.exp2(s_scr[prv, m, srow:srow + rows, :] - mx_prev)
                ps = jnp.sum(p, axis=0, keepdims=True)
                pv = _dot(v_ref[:, r0:r0 + rows], p.astype(BF16))
                lsum = ps if lsum is None else lsum + ps
                acc = pv if acc is None else acc + pv
            mx_scr[cur, m] = jnp.broadcast_to(mx_new, (SUBLANES, tq))
            o_maps.append(acc / lsum)
        o_ref[...] = _finish_head(o_maps[0], o_maps[1], lam, sg_ref, lam_init)

    for cur in range(2):
        pl.when(step % 2 == cur)(functools.partial(body, cur, 1 - cur))


def _attn_pipe_call(lam_p, subln_col, layer, qT, k, vT, lam_init, *, n_batch, seq, ctx_len):
    n_x = n_batch * seq
    ctx_blk0 = n_x // ctx_len
    tq = TQ
    n_q = seq // tq
    total = n_batch * DA_HEADS * n_q

    def unit(t):
        return t // (DA_HEADS * n_q), (t // n_q) % DA_HEADS, t % n_q

    def cur(s):
        return unit(jnp.minimum(s, total - 1))

    def prv(s):
        return unit(jnp.maximum(s - 1, 0))

    def q_map(s):
        b, h, j = cur(s)
        return h, b * n_q + j

    def o_map(s):
        b, h, j = prv(s)
        return b * n_q + j, h

    in_specs = [
        _layer_resident(lam_p, layer),
        _layer_resident(subln_col, layer),
        pl.BlockSpec((HEAD_W, tq), q_map),
        pl.BlockSpec((seq, HEAD_W), lambda s: (cur(s)[0], cur(s)[1])),
        pl.BlockSpec((ctx_len, HEAD_W), lambda s: (ctx_blk0 + cur(s)[0], cur(s)[1])),
        pl.BlockSpec((HEAD_W, seq), lambda s: (prv(s)[1], prv(s)[0])),
        pl.BlockSpec((HEAD_W, ctx_len), lambda s: (prv(s)[1], ctx_blk0 + prv(s)[0])),
    ]
    n_keys = seq + ctx_len
    kern = functools.partial(_attn_pipe_kernel, lam_init=lam_init, tq=tq)
    return pl.pallas_call(
        kern,
        grid=(total + 1,),
        in_specs=in_specs,
        out_specs=pl.BlockSpec((tq, HEAD_W), o_map),
        out_shape=jax.ShapeDtypeStruct((n_x, DA_W), BF16),
        scratch_shapes=[pltpu.VMEM((2, 2, n_keys, tq), F32),
                        pltpu.VMEM((2, 2, SUBLANES, tq), F32)],
        compiler_params=_cparams(1),
        name="diff_attn_latent",
    )(lam_p, subln_col, qT, k, k, vT, vT)


def _mix_kernel(x_ref, mod_ref, hx_ref, pa_ref, pap_ref, pan_ref, pb_ref, hd_ref, hdp_ref, hdn_ref, o_ref,
                wg_ref, caw_ref, wa_ref, sw_ref, sb_ref, wb_ref, wc_ref, dw_ref, db_ref, lg_ref, lb_ref,
                wd_ref, wo_ref, out_ref, cxe, hde, hsh, gsc, bin_scr, yd_scr, *, tiles_per_seq, tm):
    d = x_ref.shape[1]
    in_seq = pl.program_id(0) % tiles_per_seq
    prev_zero = in_seq == 0
    next_zero = in_seq == tiles_per_seq - 1
    hx = hx_ref[...]

    def fill(ext, main, prev, nxt):
        ext[HALO:HALO + tm, :] = main.astype(F32)
        ext[0:HALO, :] = jnp.where(prev_zero, 0.0, prev.astype(F32))
        ext[HALO + tm:2 * HALO + tm, :] = jnp.where(next_zero, 0.0, nxt.astype(F32))

    def dwconv(ext, shifted, w_ref, taps, r0, rows):
        acc = None
        for j in range(taps):
            a, r = divmod(HALO + r0 + j - taps // 2, SUBLANES)
            if r == 0 or shifted is None:
                win = ext[pl.ds(a * SUBLANES + r, rows), :]
            else:
                win = shifted[r - 1, pl.ds(a * SUBLANES, rows), :]
            term = w_ref[j:j + 1, :] * win
            acc = term if acc is None else acc + term
        return acc

    def conv_a_task(r):
        def run():
            if r == 0:
                fill(cxe, pa_ref[:, CONV_A_W:], pap_ref[:, CONV_A_W:], pan_ref[:, CONV_A_W:])
            rs = slice(r * CONV_RB, (r + 1) * CONV_RB)
            y = pa_ref[rs, 0:CONV_A_W].astype(F32) * dwconv(cxe, None, caw_ref, CONV_A_K, r * CONV_RB, CONV_RB)
            bin_scr[0, rs, :] = y.astype(BF16)
        return run

    def sgu_task(n):
        def run():
            lane = lax.broadcasted_iota(jnp.int32, (CHUNK, SGU_W), 1)
            rs = slice(n * CHUNK, (n + 1) * CHUNK)
            vn = pb_ref[rs, SGU_W:]
            vbd = jnp.concatenate(
                [jnp.where(lane // SGU_GROUP_W == gi, vn, jnp.zeros_like(vn)) for gi in range(SGU_GROUPS)],
                axis=0)
            s = _dot(sw_ref[...], vbd) + sb_ref[...]
            bin_scr[1, rs, :] = (pb_ref[rs, 0:SGU_W].astype(F32) * s).astype(BF16)
        return run

    def conf_fill_task():
        fill(hde, hd_ref[...], hdp_ref[...], hdn_ref[...])
        n_sh = hsh.shape[1]
        for r in range(1, SUBLANES):
            hsh[r - 1, :, :] = hde[pl.ds(r, n_sh), :]

    def conf_task(r):
        def run():
            hcv = dwconv(hde, hsh, dw_ref, CONF_K, r * CONV_RB, CONV_RB) + db_ref[...]
            mu = jnp.mean(hcv, axis=-1, keepdims=True)
            hc = hcv - mu
            hn = hc * lax.rsqrt(jnp.mean(hc * hc, axis=-1, keepdims=True) + EPS)
            hn = hn * lg_ref[...] + lb_ref[...]
            bin_scr[2, r * CONV_RB:(r + 1) * CONV_RB, :] = (hn * jax.nn.sigmoid(hn)).astype(BF16)
        return run

    def gate_task(j, half):
        def run():
            cols = slice(half * (d // 2), (half + 1) * (d // 2))
            wcols = slice(j * d + cols.start, j * d + cols.stop)
            gsc[j, :, cols] = jax.nn.sigmoid(_dot(hx, wg_ref[:, wcols]))
        return run

    n_rb = tm // CONV_RB
    n_pieces = 2 * N_BRANCH
    conf_fill_task()
    for r in range(n_rb):
        for p in range(r * n_pieces // n_rb, (r + 1) * n_pieces // n_rb):
            gate_task(p // 2, p % 2)()
        conf_task(r)()
        rs = slice(r * CONV_RB, (r + 1) * CONV_RB)
        yd_scr[rs, :] = _dot(bin_scr[2, rs, :], wd_ref[...])
    for r in range(n_rb):
        conv_a_task(r)()
    merged = gsc[0] * _dot(bin_scr[0], wa_ref[...])
    for n in range(tm // CHUNK):
        sgu_task(n)()
    merged = merged + gsc[1] * _dot(bin_scr[1], wb_ref[...])
    merged = merged + gsc[2] * _dot(o_ref[...], wc_ref[...])
    merged = merged + gsc[3] * yd_scr[...]
    y = _dot(merged.astype(BF16), wo_ref[...])
    out_ref[...] = x_ref[...] + mod_ref[5:6, :] * y


def _mix_call(xs, mods, layer, hx, pa, pb, hd, o, wts, *, n_batch, seq, ctx_len, context):
    ntok, d = xs.shape
    n_x = n_batch * seq
    if context:
        tm = ctx_len
        tiles_per_seq = 1
        n_tiles = n_batch
        blk0 = n_x // tm
        mod_map = lambda i: (layer, n_batch, 0, 0)
    else:
        tm = TM_MIX
        tiles_per_seq = seq // tm
        n_tiles = n_x // tm
        blk0 = 0
        mod_map = lambda i: (layer, i // tiles_per_seq, 0, 0)
    assert tm % CONV_RB == 0 and tm % CHUNK == 0 and seq % tm == 0 and n_x % tm == 0
    hpt = tm // HALO
    n_halo_blocks = ntok // HALO
    row = lambda i: (blk0 + i, 0)
    prev = lambda i: (jnp.maximum((blk0 + i) * hpt - 1, 0), 0)
    nxt = lambda i: (jnp.minimum((blk0 + i + 1) * hpt, n_halo_blocks - 1), 0)
    in_specs = [
        pl.BlockSpec((tm, d), row),
        pl.BlockSpec((None, None, N_MOD, d), mod_map),
        pl.BlockSpec((tm, d), row),
        pl.BlockSpec((tm, 2 * CONV_A_W), row),
        pl.BlockSpec((HALO, 2 * CONV_A_W), prev),
        pl.BlockSpec((HALO, 2 * CONV_A_W), nxt),
        pl.BlockSpec((tm, 2 * SGU_W), row),
        pl.BlockSpec((tm, CONF_W), row),
        pl.BlockSpec((HALO, CONF_W), prev),
        pl.BlockSpec((HALO, CONF_W), nxt),
        pl.BlockSpec((tm, DA_W), lambda i: (i, 0)),
    ]
    in_specs += [_layer_resident(w, layer) for w in wts]
    kern = functools.partial(_mix_kernel, tiles_per_seq=tiles_per_seq, tm=tm)
    ext_rows = tm + 2 * HALO
    return pl.pallas_call(
        kern,
        grid=(n_tiles,),
        in_specs=in_specs,
        out_specs=pl.BlockSpec((tm, d), lambda i: (i, 0)),
        out_shape=jax.ShapeDtypeStruct((n_tiles * tm, d), F32),
        scratch_shapes=[pltpu.VMEM((ext_rows, CONV_A_W), F32),
                        pltpu.VMEM((ext_rows, CONF_W), F32),
                        pltpu.VMEM((SUBLANES - 1, ext_rows - SUBLANES, CONF_W), F32),
                        pltpu.VMEM((N_BRANCH, tm, d), F32),
                        pltpu.VMEM((N_BRANCH - 1, tm, CONV_A_W), BF16),
                        pltpu.VMEM((tm, d), F32)],
        compiler_params=_cparams(1),
        name="mixer_merge_ctx" if context else "mixer_merge",
    )(xs, mods, hx, pa, pa, pa, pb, hd, hd, hd, o, *wts)


def _rope_tables(seq, pad_rows):
    n_rows = seq // GRID_W
    row = np.repeat(np.arange(n_rows, dtype=np.float32), GRID_W)
    col = np.tile(np.arange(GRID_W, dtype=np.float32), n_rows)
    inv = (np.float32(ROPE_BASE) ** (-np.arange(ROPE_NF, dtype=np.float32) / np.float32(ROPE_NF))).astype(np.float32)
    ar = (row[:, None] * inv).astype(np.float32)
    ac = (col[:, None] * inv).astype(np.float32)
    cos64 = np.concatenate([np.cos(ar), np.cos(ar), np.cos(ac), np.cos(ac)], axis=1)
    sin64 = np.concatenate([-np.sin(ar), np.sin(ar), -np.sin(ac), np.sin(ac)], axis=1)
    cos_t = np.concatenate([cos64, cos64], axis=1)
    sin_t = np.concatenate([sin64, sin64], axis=1)
    cos_t = np.concatenate([cos_t, np.ones((pad_rows, HEAD_W), np.float32)], axis=0)
    sin_t = np.concatenate([sin_t, np.zeros((pad_rows, HEAD_W), np.float32)], axis=0)
    return jnp.asarray(cos_t, F32), jnp.asarray(sin_t, F32)


def kernel(x, c, ctx, c_ctx, w_ada, b_ada, norm_g, ffn1_w1, ffn1_w3, ffn1_w2, ffn2_w1, ffn2_w3, ffn2_w2, w_in, conv_a_w, w_a_out, sgu_w, sgu_b, w_b_out, lam_p, subln_g, w_c_out, conf_dw, conf_db, conf_ln_g, conf_ln_b, w_d_out, w_o, final_g):
    n_batch, seq, d = x.shape
    ctx_len = ctx.shape[1]
    depth = w_ada.shape[0]
    n_x = n_batch * seq
    n_c = n_batch * ctx_len
    ntok = n_x + n_c
    assert n_batch + 1 <= MOD_ROWS and seq % TM_TOK == 0 and n_c % TM_TOK == 0 and seq % TQ == 0 and seq % TM_MIX == 0
    assert seq % GRID_W == 0 and w_in.shape[2] == G_OFF + N_BRANCH * d
    dims = dict(n_batch=n_batch, seq=seq)

    bf = lambda a: a.astype(BF16)
    f1w1, f1w3, f1w2 = bf(ffn1_w1), bf(ffn1_w3), bf(ffn1_w2)
    f2w1, f2w3, f2w2 = bf(ffn2_w1), bf(ffn2_w3), bf(ffn2_w2)
    w_in_b = bf(w_in[:, :, :G_OFF])
    sw_cat = bf(jnp.transpose(sgu_w, (0, 2, 1, 3)).reshape(depth, CHUNK, SGU_GROUPS * CHUNK))
    sb_full = jnp.repeat(jnp.transpose(sgu_b, (0, 2, 1)), SGU_GROUP_W, axis=2)
    mix_wts = (bf(w_in[:, :, G_OFF:]), conv_a_w, bf(w_a_out), sw_cat, sb_full, bf(w_b_out), bf(w_c_out), conf_dw,
               conf_db.reshape(depth, 1, CONF_W), conf_ln_g.reshape(depth, 1, CONF_W),
               conf_ln_b.reshape(depth, 1, CONF_W), bf(w_d_out), bf(w_o))
    subln_col = subln_g.reshape(depth, HEAD_W, 1)

    cvec = jnp.zeros((MOD_ROWS, d), F32).at[:n_batch].set(c).at[n_batch].set(c_ctx)
    mods = _ada_call(cvec, w_ada, b_ada).reshape(depth, MOD_ROWS, N_MOD, d)
    cos_t, sin_t = _rope_tables(seq, TM_TOK)

    xs = None
    for l in range(depth):
        last = l == depth - 1
        lam_init = 0.8 - 0.6 * float(np.exp(-0.3 * l))
        if l == 0:
            xs = _ffn_call(x.reshape(n_x, d), ctx.reshape(n_c, d), mods, l, 0, 0, norm_g, f1w1, f1w3, f1w2,
                           None, n_out_rows=ntok, **dims)
        else:
            xs = _ffn_call(xs, None, mods, l, 0, 0, norm_g, f1w1, f1w3, f1w2, None, n_out_rows=ntok, **dims)
        hx, pa, pb, qT, k, vT, hd = _proj_call(xs, mods, l, norm_g, w_in_b, cos_t, sin_t, **dims)
        o_x = _attn_pipe_call(lam_p, subln_col, l, qT, k, vT, lam_init, ctx_len=ctx_len, **dims)
        xs_x = _mix_call(xs, mods, l, hx, pa, pb, hd, o_x, mix_wts, ctx_len=ctx_len, context=False, **dims)
        if last:
            xs = _ffn_call(xs_x, None, mods, l, 2, 6, norm_g, f2w1, f2w3, f2w2, final_g.reshape(1, d),
                           n_out_rows=n_x, **dims)
        else:
            o_c = _attn_ctx_call(lam_p, subln_col, l, qT, k, vT, lam_init, ctx_len=ctx_len, **dims)
            xs_c = _mix_call(xs, mods, l, hx, pa, pb, hd, o_c, mix_wts, ctx_len=ctx_len, context=True, **dims)
            xs = _ffn_call(xs_x, xs_c, mods, l, 2, 6, norm_g, f2w1, f2w3, f2w2, None, n_out_rows=ntok, **dims)
    return xs.reshape(n_batch, seq, d)
```

```python
import functools
import math

import jax
import jax.numpy as jnp
import numpy as np
from jax import lax
from jax.experimental import pallas as pl
from jax.experimental.pallas import tpu as pltpu

F32 = jnp.float32
BF16 = jnp.bfloat16

EPS = 1e-6
GRID_W = 64
N_MOD = 9
ROPE_BASE = 10000.0

CONV_A_W = 256
CONV_A_K = 3
SGU_W = 256
SGU_GROUPS = 4
SGU_GROUP_W = SGU_W // SGU_GROUPS
CHUNK = 128
DA_HEADS = 4
DA_HEAD_DIM = 64
HEAD_W = 2 * DA_HEAD_DIM
DA_W = DA_HEADS * HEAD_W
ROPE_NF = DA_HEAD_DIM // 4
CONF_W = 256
CONF_K = 31
N_BRANCH = 4
Q_SCALE = DA_HEAD_DIM ** -0.5 * math.log2(math.e)

A_OFF = 0
B_OFF = A_OFF + 3 * CONV_A_W
Q_OFF = B_OFF + 2 * SGU_W
K_OFF = Q_OFF + DA_W
V_OFF = K_OFF + DA_W
D_OFF = V_OFF + DA_W
G_OFF = D_OFF + 2 * CONF_W

SUBLANES = 8
TM_TOK = 512
TM_FFN = 1024
FFN_RB = 256
PROJ_RB = 256
TM_MIX = 512
TQ = 512
KB = 256
HALO = 16
CONV_RB = 128
MOD_ROWS = 8
VMEM_LIMIT = 56 * 1024 * 1024


def _cparams(n_grid):
    return pltpu.CompilerParams(dimension_semantics=("arbitrary",) * n_grid,
                                vmem_limit_bytes=VMEM_LIMIT)


def _resident(shape):
    nd = len(shape)
    return pl.BlockSpec(shape, lambda *_: (0,) * nd, pipeline_mode=pl.Buffered(1))


def _layer_resident(arr, layer):
    nd = arr.ndim - 1
    return pl.BlockSpec((None,) + arr.shape[1:], lambda *_: (layer,) + (0,) * nd,
                        pipeline_mode=pl.Buffered(1))


def _dot(a, b):
    return jnp.dot(a, b, preferred_element_type=F32)


def _modnorm(x, g, shift, scale):
    y = x * lax.rsqrt(jnp.mean(x * x, axis=-1, keepdims=True) + EPS)
    return (y * g) * (1.0 + scale) + shift


def _ada_kernel(c_ref, w_ref, b_ref, o_ref):
    c = c_ref[...]
    s = (c * jax.nn.sigmoid(c)).astype(BF16)
    o_ref[...] = _dot(s, w_ref[...].astype(BF16)) + b_ref[...]


def _ada_call(cvec, w_ada, b_ada):
    depth, d, n = w_ada.shape
    tn = math.gcd(n, 1536)
    return pl.pallas_call(
        _ada_kernel,
        grid=(depth, n // tn),
        in_specs=[
            pl.BlockSpec((MOD_ROWS, d), lambda l, j: (0, 0)),
            pl.BlockSpec((None, d, tn), lambda l, j: (l, 0, j)),
            pl.BlockSpec((None, 1, tn), lambda l, j: (l, 0, j)),
        ],
        out_specs=pl.BlockSpec((None, MOD_ROWS, tn), lambda l, j: (l, 0, j)),
        out_shape=jax.ShapeDtypeStruct((depth, MOD_ROWS, n), F32),
        compiler_params=_cparams(2),
        name="ada_mod",
    )(cvec, w_ada, b_ada.reshape(depth, 1, n))


def _ffn_chunks(d_ff):
    tiles = d_ff // 256
    assert tiles * 256 == d_ff and tiles >= 2
    cut = (tiles + 1) // 2 * 256
    return ((0, cut), (cut, d_ff))


def _ffn_kernel(*refs, n_x_tiles, two_inputs, final, norm_row, mod_base, d_ff):
    if two_inputs:
        x_ref, c_ref, *refs = refs
    else:
        x_ref, *refs = refs
    if final:
        mod_ref, g_ref, w1_ref, w3_ref, w2_ref, fg_ref, o_ref = refs
    else:
        mod_ref, g_ref, w1_ref, w3_ref, w2_ref, o_ref = refs
    shift = mod_ref[mod_base:mod_base + 1, :]
    scale = mod_ref[mod_base + 1:mod_base + 2, :]
    gate = mod_ref[mod_base + 2:mod_base + 3, :]
    for r0 in range(0, x_ref.shape[0], FFN_RB):
        rs = slice(r0, r0 + FFN_RB)
        x = x_ref[rs, :]
        if two_inputs:
            x = jnp.where(pl.program_id(0) < n_x_tiles, x, c_ref[rs, :])
        h = _modnorm(x, g_ref[norm_row:norm_row + 1, :], shift, scale).astype(BF16)
        acc = None
        for lo, hi in _ffn_chunks(d_ff):
            a = _dot(h, w1_ref[:, lo:hi])
            b = _dot(h, w3_ref[:, lo:hi])
            gch = (a * jax.nn.sigmoid(a) * b).astype(BF16)
            part = _dot(gch, w2_ref[lo:hi, :])
            acc = part if acc is None else acc + part
        y = x + (0.5 * gate) * acc
        if final:
            y = y * lax.rsqrt(jnp.mean(y * y, axis=-1, keepdims=True) + EPS) * fg_ref[...]
        o_ref[rs, :] = y


def _mod_row(i, n_x_tiles, tiles_per_batch, n_batch):
    return jnp.where(i < n_x_tiles, i // tiles_per_batch, n_batch)


def _ffn_call(xs, ctx2d, mods, layer, norm_row, mod_base, norm_g, w1, w3, w2, final_g, *,
              n_batch, seq, n_out_rows):
    d = xs.shape[1]
    d_ff = w1.shape[2]
    two_inputs = ctx2d is not None
    tm = TM_TOK if n_out_rows % TM_FFN or seq % TM_FFN or (two_inputs and ctx2d.shape[0] % TM_FFN) else TM_FFN
    n_x_tiles = n_batch * seq // tm
    n_tiles = n_out_rows // tm
    tiles_per_batch = seq // tm
    final = final_g is not None
    in_specs = []
    args = []
    if two_inputs:
        in_specs.append(pl.BlockSpec((tm, d), lambda i: (jnp.minimum(i, n_x_tiles - 1), 0)))
        in_specs.append(pl.BlockSpec((tm, d), lambda i: (jnp.maximum(i - n_x_tiles, 0), 0)))
        args += [xs, ctx2d]
    else:
        in_specs.append(pl.BlockSpec((tm, d), lambda i: (i, 0)))
        args.append(xs)
    in_specs += [
        pl.BlockSpec((None, None, N_MOD, d),
                     lambda i: (layer, _mod_row(i, n_x_tiles, tiles_per_batch, n_batch), 0, 0)),
        _layer_resident(norm_g, layer),
        _layer_resident(w1, layer),
        _layer_resident(w3, layer),
        _layer_resident(w2, layer),
    ]
    args += [mods, norm_g, w1, w3, w2]
    if final:
        in_specs.append(_resident((1, d)))
        args.append(final_g)
    kern = functools.partial(_ffn_kernel, n_x_tiles=n_x_tiles, two_inputs=two_inputs, final=final,
                             norm_row=norm_row, mod_base=mod_base, d_ff=d_ff)
    return pl.pallas_call(
        kern,
        grid=(n_tiles,),
        in_specs=in_specs,
        out_specs=pl.BlockSpec((tm, d), lambda i: (i, 0)),
        out_shape=jax.ShapeDtypeStruct((n_out_rows, d), F32),
        compiler_params=_cparams(1),
        name="ffn",
    )(*args)


def _gelu_tanh(x):
    c = math.sqrt(2.0 / math.pi)
    return x * (0.5 * (1.0 + jnp.tanh(c * (x + 0.044715 * (x * x * x)))))


def _proj_kernel(x_ref, mod_ref, g_ref, w_ref, cos_ref, sin_ref,
                 hx_ref, pa_ref, pb_ref, qT_ref, k_ref, vT_ref, hd_ref):
    lane = lax.broadcasted_iota(jnp.int32, (PROJ_RB, HEAD_W), 1)
    first_half = (lane % (2 * ROPE_NF)) < ROPE_NF
    for r0 in range(0, x_ref.shape[0], PROJ_RB):
        rs = slice(r0, r0 + PROJ_RB)
        h = _modnorm(x_ref[rs, :], g_ref[1:2, :], mod_ref[3:4, :], mod_ref[4:5, :]).astype(BF16)
        hx_ref[rs, :] = h

        def proj(lo, hi):
            return _dot(h, w_ref[:, lo:hi])

        pa = proj(A_OFF, B_OFF)
        pa_ref[rs, 0:CONV_A_W] = pa[:, 0:CONV_A_W].astype(BF16)
        pa_ref[rs, CONV_A_W:] = (pa[:, CONV_A_W:2 * CONV_A_W] * pa[:, 2 * CONV_A_W:]).astype(BF16)

        z = _gelu_tanh(proj(B_OFF, Q_OFF))
        v = z[:, SGU_W:]
        vc = v - jnp.mean(v, axis=-1, keepdims=True)
        vn = vc * lax.rsqrt(jnp.mean(vc * vc, axis=-1, keepdims=True) + EPS)
        pb_ref[rs, 0:SGU_W] = z[:, 0:SGU_W].astype(BF16)
        pb_ref[rs, SGU_W:] = vn.astype(BF16)

        cos = cos_ref[rs, :]
        sin = sin_ref[rs, :]

        def rope(blk):
            partner = jnp.where(first_half, pltpu.roll(blk, HEAD_W - ROPE_NF, 1), pltpu.roll(blk, ROPE_NF, 1))
            return blk * cos + partner * sin

        q = proj(Q_OFF, K_OFF)
        for hh in range(DA_HEADS):
            sl = slice(hh * HEAD_W, (hh + 1) * HEAD_W)
            qr = rope(q[:, sl]) * Q_SCALE
            qT_ref[sl, rs] = qr.T.astype(BF16)
        k = proj(K_OFF, V_OFF)
        for hh in range(DA_HEADS):
            sl = slice(hh * HEAD_W, (hh + 1) * HEAD_W)
            k_ref[rs, sl] = rope(k[:, sl]).astype(BF16)
        vv = proj(V_OFF, D_OFF)
        for hh in range(DA_HEADS):
            sl = slice(hh * HEAD_W, (hh + 1) * HEAD_W)
            vT_ref[sl, rs] = vv[:, sl].T.astype(BF16)

        zd = proj(D_OFF, G_OFF)
        hd_ref[rs, :] = (zd[:, 0:CONF_W] * jax.nn.sigmoid(zd[:, CONF_W:])).astype(BF16)


def _proj_call(xs, mods, layer, norm_g, w_in, cos_t, sin_t, *, n_batch, seq):
    ntok, d = xs.shape
    tm = TM_TOK
    n_x_tiles = n_batch * seq // tm
    tiles_per_batch = seq // tm
    row = lambda i: (i, 0)
    col = lambda i: (0, i)
    rope_row = lambda i: (jnp.where(i < n_x_tiles, i % tiles_per_batch, tiles_per_batch), 0)
    outs = pl.pallas_call(
        _proj_kernel,
        grid=(ntok // tm,),
        in_specs=[
            pl.BlockSpec((tm, d), row),
            pl.BlockSpec((None, None, N_MOD, d),
                         lambda i: (layer, _mod_row(i, n_x_tiles, tiles_per_batch, n_batch), 0, 0)),
            _layer_resident(norm_g, layer),
            pl.BlockSpec((None, d, G_OFF), lambda i: (layer, 0, 0), pipeline_mode=pl.Buffered(1)),
            pl.BlockSpec((tm, HEAD_W), rope_row),
            pl.BlockSpec((tm, HEAD_W), rope_row),
        ],
        out_specs=[
            pl.BlockSpec((tm, d), row),
            pl.BlockSpec((tm, 2 * CONV_A_W), row),
            pl.BlockSpec((tm, 2 * SGU_W), row),
            pl.BlockSpec((DA_W, tm), col),
            pl.BlockSpec((tm, DA_W), row),
            pl.BlockSpec((DA_W, tm), col),
            pl.BlockSpec((tm, CONF_W), row),
        ],
        out_shape=[
            jax.ShapeDtypeStruct((ntok, d), BF16),
            jax.ShapeDtypeStruct((ntok, 2 * CONV_A_W), BF16),
            jax.ShapeDtypeStruct((ntok, 2 * SGU_W), BF16),
            jax.ShapeDtypeStruct((DA_W, ntok), BF16),
            jax.ShapeDtypeStruct((ntok, DA_W), BF16),
            jax.ShapeDtypeStruct((DA_W, ntok), BF16),
            jax.ShapeDtypeStruct((ntok, CONF_W), BF16),
        ],
        compiler_params=_cparams(1),
        name="mixer_proj",
    )(xs, mods, norm_g, w_in, cos_t, sin_t)
    return outs


def _lambda(lam_ref, lam_init):
    lp = lam_ref[...]
    return (jnp.exp(jnp.sum(lp[0:1, :] * lp[1:2, :], axis=-1, keepdims=True))
            - jnp.exp(jnp.sum(lp[2:3, :] * lp[3:4, :], axis=-1, keepdims=True)) + lam_init)


def _finish_head(o0, o1, lam, sg_ref, lam_init):
    oh = o0 - lam * o1
    y = oh * lax.rsqrt(jnp.mean(oh * oh, axis=0, keepdims=True) + EPS)
    return (y * (sg_ref[...] * (1.0 - lam_init))).T.astype(BF16)


def _attn_ctx_kernel(lam_ref, sg_ref, qT_ref, k_ref, vT_ref, o_ref, *, lam_init):
    tq = qT_ref.shape[1]
    lam = _lambda(lam_ref, lam_init)
    sub = lax.broadcasted_iota(jnp.int32, (HEAD_W, tq), 0)
    for hh in range(DA_HEADS):
        hs = slice(hh * HEAD_W, (hh + 1) * HEAD_W)
        qh = qT_ref[hs, :]
        o_maps = []
        for m in range(2):
            in_map = (sub < DA_HEAD_DIM) if m == 0 else (sub >= DA_HEAD_DIM)
            qm = jnp.where(in_map, qh, jnp.zeros_like(qh))
            s = _dot(k_ref[:, hs], qm)
            p = jnp.exp2(s - jnp.max(s, axis=0, keepdims=True))
            o_maps.append(_dot(vT_ref[hs, :], p.astype(BF16)) / jnp.sum(p, axis=0, keepdims=True))
        o_ref[:, hs] = _finish_head(o_maps[0], o_maps[1], lam, sg_ref, lam_init)


def _attn_ctx_call(lam_p, subln_col, layer, qT, k, vT, lam_init, *, n_batch, seq, ctx_len):
    blk0 = n_batch * seq // ctx_len
    kern = functools.partial(_attn_ctx_kernel, lam_init=lam_init)
    return pl.pallas_call(
        kern,
        grid=(n_batch,),
        in_specs=[
            _layer_resident(lam_p, layer),
            _layer_resident(subln_col, layer),
            pl.BlockSpec((DA_W, ctx_len), lambda b: (0, blk0 + b)),
            pl.BlockSpec((ctx_len, DA_W), lambda b: (blk0 + b, 0)),
            pl.BlockSpec((DA_W, ctx_len), lambda b: (0, blk0 + b)),
        ],
        out_specs=pl.BlockSpec((ctx_len, DA_W), lambda b: (b, 0)),
        out_shape=jax.ShapeDtypeStruct((n_batch * ctx_len, DA_W), BF16),
        compiler_params=_cparams(1),
        name="diff_attn_ctx",
    )(lam_p, subln_col, qT, k, vT)


def _attn_pipe_kernel(lam_ref, sg_ref, qT_ref, kl_ref, kc_ref, vl_ref, vc_ref, o_ref, s_scr, mx_scr, *,
                      lam_init, tq):
    step = pl.program_id(0)
    seq = kl_ref.shape[0]
    n_ctx = kc_ref.shape[0]
    blocks = [(kl_ref, vl_ref, r0, KB, r0) for r0 in range(0, seq, KB)]
    blocks.append((kc_ref, vc_ref, 0, n_ctx, seq))

    @pl.when(step == 0)
    def _():
        s_scr[1] = jnp.zeros(s_scr.shape[1:], F32)
        mx_scr[1] = jnp.zeros(mx_scr.shape[1:], F32)

    lam = _lambda(lam_ref, lam_init)

    def body(cur, prv):
        sub = lax.broadcasted_iota(jnp.int32, (HEAD_W, tq), 0)
        qh = qT_ref[...]
        o_maps = []
        for m in range(2):
            in_map = (sub < DA_HEAD_DIM) if m == 0 else (sub >= DA_HEAD_DIM)
            qm = jnp.where(in_map, qh, jnp.zeros_like(qh))
            mx_prev = mx_scr[prv, m, 0:1, :]
            mx_new = None
            lsum = acc = None
            for k_ref, v_ref, r0, rows, srow in blocks:
                s = _dot(k_ref[r0:r0 + rows, :], qm)
                s_scr[cur, m, srow:srow + rows, :] = s
                bm = jnp.max(s, axis=0, keepdims=True)
                mx_new = bm if mx_new is None else jnp.maximum(mx_new, bm)
                p = jnp.exp2(s_scr[prv, m, srow:srow + rows, :] - mx_prev)
                ps = jnp.sum(p, axis=0, keepdims=True)
                pv = _dot(v_ref[:, r0:r0 + rows], p.astype(BF16))
                lsum = ps if lsum is None else lsum + ps
                acc = pv if acc is None else acc + pv
            mx_scr[cur, m] = jnp.broadcast_to(mx_new, (SUBLANES, tq))
            o_maps.append(acc / lsum)
        o_ref[...] = _finish_head(o_maps[0], o_maps[1], lam, sg_ref, lam_init)

    for cur in range(2):
        pl.when(step % 2 == cur)(functools.partial(body, cur, 1 - cur))


def _attn_pipe_call(lam_p, subln_col, layer, qT, k, vT, lam_init, *, n_batch, seq, ctx_len):
    n_x = n_batch * seq
    ctx_blk0 = n_x // ctx_len
    tq = TQ
    n_q = seq // tq
    total = n_batch * DA_HEADS * n_q

    def unit(t):
        return t // (DA_HEADS * n_q), (t // n_q) % DA_HEADS, t % n_q

    def cur(s):
        return unit(jnp.minimum(s, total - 1))

    def prv(s):
        return unit(jnp.maximum(s - 1, 0))

    def q_map(s):
        b, h, j = cur(s)
        return h, b * n_q + j

    def o_map(s):
        b, h, j = prv(s)
        return b * n_q + j, h

    in_specs = [
        _layer_resident(lam_p, layer),
        _layer_resident(subln_col, layer),
        pl.BlockSpec((HEAD_W, tq), q_map),
        pl.BlockSpec((seq, HEAD_W), lambda s: (cur(s)[0], cur(s)[1])),
        pl.BlockSpec((ctx_len, HEAD_W), lambda s: (ctx_blk0 + cur(s)[0], cur(s)[1])),
        pl.BlockSpec((HEAD_W, seq), lambda s: (prv(s)[1], prv(s)[0])),
        pl.BlockSpec((HEAD_W, ctx_len), lambda s: (prv(s)[1], ctx_blk0 + prv(s)[0])),
    ]
    n_keys = seq + ctx_len
    kern = functools.partial(_attn_pipe_kernel, lam_init=lam_init, tq=tq)
    return pl.pallas_call(
        kern,
        grid=(total + 1,),
        in_specs=in_specs,
        out_specs=pl.BlockSpec((tq, HEAD_W), o_map),
        out_shape=jax.ShapeDtypeStruct((n_x, DA_W), BF16),
        scratch_shapes=[pltpu.VMEM((2, 2, n_keys, tq), F32),
                        pltpu.VMEM((2, 2, SUBLANES, tq), F32)],
        compiler_params=_cparams(1),
        name="diff_attn_latent",
    )(lam_p, subln_col, qT, k, k, vT, vT)


def _mix_kernel(x_ref, mod_ref, hx_ref, pa_ref, pap_ref, pan_ref, pb_ref, hd_ref, hdp_ref, hdn_ref, o_ref,
                wg_ref, caw_ref, wa_ref, sw_ref, sb_ref, wb_ref, wc_ref, dw_ref, db_ref, lg_ref, lb_ref,
                wd_ref, wo_ref, out_ref, cxe, hde, hsh, gsc, bin_scr, yd_scr, *, tiles_per_seq, tm):
    d = x_ref.shape[1]
    in_seq = pl.program_id(0) % tiles_per_seq
    prev_zero = in_seq == 0
    next_zero = in_seq == tiles_per_seq - 1
    hx = hx_ref[...]

    def fill(ext, main, prev, nxt):
        ext[HALO:HALO + tm, :] = main.astype(F32)
        ext[0:HALO, :] = jnp.where(prev_zero, 0.0, prev.astype(F32))
        ext[HALO + tm:2 * HALO + tm, :] = jnp.where(next_zero, 0.0, nxt.astype(F32))

    def dwconv(ext, shifted, w_ref, taps, r0, rows):
        acc = None
        for j in range(taps):
            a, r = divmod(HALO + r0 + j - taps // 2, SUBLANES)
            if r == 0 or shifted is None:
                win = ext[pl.ds(a * SUBLANES + r, rows), :]
            else:
                win = shifted[r - 1, pl.ds(a * SUBLANES, rows), :]
            term = w_ref[j:j + 1, :] * win
            acc = term if acc is None else acc + term
        return acc

    def conv_a_task(r):
        def run():
            if r == 0:
                fill(cxe, pa_ref[:, CONV_A_W:], pap_ref[:, CONV_A_W:], pan_ref[:, CONV_A_W:])
            rs = slice(r * CONV_RB, (r + 1) * CONV_RB)
            y = pa_ref[rs, 0:CONV_A_W].astype(F32) * dwconv(cxe, None, caw_ref, CONV_A_K, r * CONV_RB, CONV_RB)
            bin_scr[0, rs, :] = y.astype(BF16)
        return run

    def sgu_task(n):
        def run():
            lane = lax.broadcasted_iota(jnp.int32, (CHUNK, SGU_W), 1)
            rs = slice(n * CHUNK, (n + 1) * CHUNK)
            vn = pb_ref[rs, SGU_W:]
            vbd = jnp.concatenate(
                [jnp.where(lane // SGU_GROUP_W == gi, vn, jnp.zeros_like(vn)) for gi in range(SGU_GROUPS)],
                axis=0)
            s = _dot(sw_ref[...], vbd) + sb_ref[...]
            bin_scr[1, rs, :] = (pb_ref[rs, 0:SGU_W].astype(F32) * s).astype(BF16)
        return run

    def conf_fill_task():
        fill(hde, hd_ref[...], hdp_ref[...], hdn_ref[...])
        n_sh = hsh.shape[1]
        for r in range(1, SUBLANES):
            hsh[r - 1, :, :] = hde[pl.ds(r, n_sh), :]

    def conf_task(r):
        def run():
            hcv = dwconv(hde, hsh, dw_ref, CONF_K, r * CONV_RB, CONV_RB) + db_ref[...]
            mu = jnp.mean(hcv, axis=-1, keepdims=True)
            hc = hcv - mu
            hn = hc * lax.rsqrt(jnp.mean(hc * hc, axis=-1, keepdims=True) + EPS)
            hn = hn * lg_ref[...] + lb_ref[...]
            bin_scr[2, r * CONV_RB:(r + 1) * CONV_RB, :] = (hn * jax.nn.sigmoid(hn)).astype(BF16)
        return run

    def gate_task(j, half):
        def run():
            cols = slice(half * (d // 2), (half + 1) * (d // 2))
            wcols = slice(j * d + cols.start, j * d + cols.stop)
            gsc[j, :, cols] = jax.nn.sigmoid(_dot(hx, wg_ref[:, wcols]))
        return run

    n_rb = tm // CONV_RB
    n_pieces = 2 * N_BRANCH
    conf_fill_task()
    for r in range(n_rb):
        for p in range(r * n_pieces // n_rb, (r + 1) * n_pieces // n_rb):
            gate_task(p // 2, p % 2)()
        conf_task(r)()
        rs = slice(r * CONV_RB, (r + 1) * CONV_RB)
        yd_scr[rs, :] = _dot(bin_scr[2, rs, :], wd_ref[...])
    for r in range(n_rb):
        conv_a_task(r)()
    merged = gsc[0] * _dot(bin_scr[0], wa_ref[...])
    for n in range(tm // CHUNK):
        sgu_task(n)()
    merged = merged + gsc[1] * _dot(bin_scr[1], wb_ref[...])
    merged = merged + gsc[2] * _dot(o_ref[...], wc_ref[...])
    merged = merged + gsc[3] * yd_scr[...]
    y = _dot(merged.astype(BF16), wo_ref[...])
    out_ref[...] = x_ref[...] + mod_ref[5:6, :] * y


def _mix_call(xs, mods, layer, hx, pa, pb, hd, o, wts, *, n_batch, seq, ctx_len, context):
    ntok, d = xs.shape
    n_x = n_batch * seq
    if context:
        tm = ctx_len
        tiles_per_seq = 1
        n_tiles = n_batch
        blk0 = n_x // tm
        mod_map = lambda i: (layer, n_batch, 0, 0)
    else:
        tm = TM_MIX
        tiles_per_seq = seq // tm
        n_tiles = n_x // tm
        blk0 = 0
        mod_map = lambda i: (layer, i // tiles_per_seq, 0, 0)
    assert tm % CONV_RB == 0 and tm % CHUNK == 0 and seq % tm == 0 and n_x % tm == 0
    hpt = tm // HALO
    n_halo_blocks = ntok // HALO
    row = lambda i: (blk0 + i, 0)
    prev = lambda i: (jnp.maximum((blk0 + i) * hpt - 1, 0), 0)
    nxt = lambda i: (jnp.minimum((blk0 + i + 1) * hpt, n_halo_blocks - 1), 0)
    in_specs = [
        pl.BlockSpec((tm, d), row),
        pl.BlockSpec((None, None, N_MOD, d), mod_map),
        pl.BlockSpec((tm, d), row),
        pl.BlockSpec((tm, 2 * CONV_A_W), row),
        pl.BlockSpec((HALO, 2 * CONV_A_W), prev),
        pl.BlockSpec((HALO, 2 * CONV_A_W), nxt),
        pl.BlockSpec((tm, 2 * SGU_W), row),
        pl.BlockSpec((tm, CONF_W), row),
        pl.BlockSpec((HALO, CONF_W), prev),
        pl.BlockSpec((HALO, CONF_W), nxt),
        pl.BlockSpec((tm, DA_W), lambda i: (i, 0)),
    ]
    in_specs += [_layer_resident(w, layer) for w in wts]
    kern = functools.partial(_mix_kernel, tiles_per_seq=tiles_per_seq, tm=tm)
    ext_rows = tm + 2 * HALO
    return pl.pallas_call(
        kern,
        grid=(n_tiles,),
        in_specs=in_specs,
        out_specs=pl.BlockSpec((tm, d), lambda i: (i, 0)),
        out_shape=jax.ShapeDtypeStruct((n_tiles * tm, d), F32),
        scratch_shapes=[pltpu.VMEM((ext_rows, CONV_A_W), F32),
                        pltpu.VMEM((ext_rows, CONF_W), F32),
                        pltpu.VMEM((SUBLANES - 1, ext_rows - SUBLANES, CONF_W), F32),
                        pltpu.VMEM((N_BRANCH, tm, d), F32),
                        pltpu.VMEM((N_BRANCH - 1, tm, CONV_A_W), BF16),
                        pltpu.VMEM((tm, d), F32)],
        compiler_params=_cparams(1),
        name="mixer_merge_ctx" if context else "mixer_merge",
    )(xs, mods, hx, pa, pa, pa, pb, hd, hd, hd, o, *wts)


def _rope_tables(seq, pad_rows):
    n_rows = seq // GRID_W
    row = np.repeat(np.arange(n_rows, dtype=np.float32), GRID_W)
    col = np.tile(np.arange(GRID_W, dtype=np.float32), n_rows)
    inv = (np.float32(ROPE_BASE) ** (-np.arange(ROPE_NF, dtype=np.float32) / np.float32(ROPE_NF))).astype(np.float32)
    ar = (row[:, None] * inv).astype(np.float32)
    ac = (col[:, None] * inv).astype(np.float32)
    cos64 = np.concatenate([np.cos(ar), np.cos(ar), np.cos(ac), np.cos(ac)], axis=1)
    sin64 = np.concatenate([-np.sin(ar), np.sin(ar), -np.sin(ac), np.sin(ac)], axis=1)
    cos_t = np.concatenate([cos64, cos64], axis=1)
    sin_t = np.concatenate([sin64, sin64], axis=1)
    cos_t = np.concatenate([cos_t, np.ones((pad_rows, HEAD_W), np.float32)], axis=0)
    sin_t = np.concatenate([sin_t, np.zeros((pad_rows, HEAD_W), np.float32)], axis=0)
    return jnp.asarray(cos_t, F32), jnp.asarray(sin_t, F32)


def kernel(x, c, ctx, c_ctx, w_ada, b_ada, norm_g, ffn1_w1, ffn1_w3, ffn1_w2, ffn2_w1, ffn2_w3, ffn2_w2, w_in, conv_a_w, w_a_out, sgu_w, sgu_b, w_b_out, lam_p, subln_g, w_c_out, conf_dw, conf_db, conf_ln_g, conf_ln_b, w_d_out, w_o, final_g):
    n_batch, seq, d = x.shape
    ctx_len = ctx.shape[1]
    depth = w_ada.shape[0]
    n_x = n_batch * seq
    n_c = n_batch * ctx_len
    ntok = n_x + n_c
    assert n_batch + 1 <= MOD_ROWS and seq % TM_TOK == 0 and n_c % TM_TOK == 0 and seq % TQ == 0 and seq % TM_MIX == 0
    assert seq % GRID_W == 0 and w_in.shape[2] == G_OFF + N_BRANCH * d
    dims = dict(n_batch=n_batch, seq=seq)

    bf = lambda a: a.astype(BF16)
    f1w1, f1w3, f1w2 = bf(ffn1_w1), bf(ffn1_w3), bf(ffn1_w2)
    f2w1, f2w3, f2w2 = bf(ffn2_w1), bf(ffn2_w3), bf(ffn2_w2)
    w_in_b = bf(w_in[:, :, :G_OFF])
    sw_cat = bf(jnp.transpose(sgu_w, (0, 2, 1, 3)).reshape(depth, CHUNK, SGU_GROUPS * CHUNK))
    sb_full = jnp.repeat(jnp.transpose(sgu_b, (0, 2, 1)), SGU_GROUP_W, axis=2)
    mix_wts = (bf(w_in[:, :, G_OFF:]), conv_a_w, bf(w_a_out), sw_cat, sb_full, bf(w_b_out), bf(w_c_out), conf_dw,
               conf_db.reshape(depth, 1, CONF_W), conf_ln_g.reshape(depth, 1, CONF_W),
               conf_ln_b.reshape(depth, 1, CONF_W), bf(w_d_out), bf(w_o))
    subln_col = subln_g.reshape(depth, HEAD_W, 1)

    cvec = jnp.zeros((MOD_ROWS, d), F32).at[:n_batch].set(c).at[n_batch].set(c_ctx)
    mods = _ada_call(cvec, w_ada, b_ada).reshape(depth, MOD_ROWS, N_MOD, d)
    cos_t, sin_t = _rope_tables(seq, TM_TOK)

    xs = None
    for l in range(depth):
        last = l == depth - 1
        lam_init = 0.8 - 0.6 * float(np.exp(-0.3 * l))
        if l == 0:
            xs = _ffn_call(x.reshape(n_x, d), ctx.reshape(n_c, d), mods, l, 0, 0, norm_g, f1w1, f1w3, f1w2,
                           None, n_out_rows=ntok, **dims)
        else:
            xs = _ffn_call(xs, None, mods, l, 0, 0, norm_g, f1w1, f1w3, f1w2, None, n_out_rows=ntok, **dims)
        hx, pa, pb, qT, k, vT, hd = _proj_call(xs, mods, l, norm_g, w_in_b, cos_t, sin_t, **dims)
        o_x = _attn_pipe_call(lam_p, subln_col, l, qT, k, vT, lam_init, ctx_len=ctx_len, **dims)
        xs_x = _mix_call(xs, mods, l, hx, pa, pb, hd, o_x, mix_wts, ctx_len=ctx_len, context=False, **dims)
        if last:
            xs = _ffn_call(xs_x, None, mods, l, 2, 6, norm_g, f2w1, f2w3, f2w2, final_g.reshape(1, d),
                           n_out_rows=n_x, **dims)
        else:
            o_c = _attn_ctx_call(lam_p, subln_col, l, qT, k, vT, lam_init, ctx_len=ctx_len, **dims)
            xs_c = _mix_call(xs, mods, l, hx, pa, pb, hd, o_c, mix_wts, ctx_len=ctx_len, context=True, **dims)
            xs = _ffn_call(xs_x, xs_c, mods, l, 2, 6, norm_g, f2w1, f2w3, f2w2, None, n_out_rows=ntok, **dims)
    return xs.reshape(n_batch, seq, d)
```

```python
import functools
import math

import jax
import jax.numpy as jnp
import numpy as np
from jax import lax
from jax.experimental import pallas as pl
from jax.experimental.pallas import tpu as pltpu

F32 = jnp.float32
BF16 = jnp.bfloat16

EPS = 1e-6
GRID_W = 64
N_MOD = 9
ROPE_BASE = 10000.0

CONV_A_W = 256
CONV_A_K = 3
SGU_W = 256
SGU_GROUPS = 4
SGU_GROUP_W = SGU_W // SGU_GROUPS
CHUNK = 128
DA_HEADS = 4
DA_HEAD_DIM = 64
HEAD_W = 2 * DA_HEAD_DIM
DA_W = DA_HEADS * HEAD_W
ROPE_NF = DA_HEAD_DIM // 4
CONF_W = 256
CONF_K = 31
N_BRANCH = 4
Q_SCALE = DA_HEAD_DIM ** -0.5 * math.log2(math.e)

A_OFF = 0
B_OFF = A_OFF + 3 * CONV_A_W
Q_OFF = B_OFF + 2 * SGU_W
K_OFF = Q_OFF + DA_W
V_OFF = K_OFF + DA_W
D_OFF = V_OFF + DA_W
G_OFF = D_OFF + 2 * CONF_W

SUBLANES = 8
TM_TOK = 512
TM_FFN = 1024
FFN_RB = 256
PROJ_RB = 256
TM_MIX = 512
TQ = 512
KB = 256
HALO = 16
CONV_RB = 128
MOD_ROWS = 8
VMEM_LIMIT = 56 * 1024 * 1024


def _cparams(n_grid):
    return pltpu.CompilerParams(dimension_semantics=("arbitrary",) * n_grid,
                                vmem_limit_bytes=VMEM_LIMIT)


def _resident(shape):
    nd = len(shape)
    return pl.BlockSpec(shape, lambda *_: (0,) * nd, pipeline_mode=pl.Buffered(1))


def _layer_resident(arr, layer):
    nd = arr.ndim - 1
    return pl.BlockSpec((None,) + arr.shape[1:], lambda *_: (layer,) + (0,) * nd,
                        pipeline_mode=pl.Buffered(1))


def _dot(a, b):
    return jnp.dot(a, b, preferred_element_type=F32)


def _modnorm(x, g, shift, scale):
    y = x * lax.rsqrt(jnp.mean(x * x, axis=-1, keepdims=True) + EPS)
    return (y * g) * (1.0 + scale) + shift


def _ada_kernel(c_ref, w_ref, b_ref, o_ref):
    c = c_ref[...]
    s = (c * jax.nn.sigmoid(c)).astype(BF16)
    o_ref[...] = _dot(s, w_ref[...].astype(BF16)) + b_ref[...]


def _ada_call(cvec, w_ada, b_ada):
    depth, d, n = w_ada.shape
    tn = math.gcd(n, 1536)
    return pl.pallas_call(
        _ada_kernel,
        grid=(depth, n // tn),
        in_specs=[
            pl.BlockSpec((MOD_ROWS, d), lambda l, j: (0, 0)),
            pl.BlockSpec((None, d, tn), lambda l, j: (l, 0, j)),
            pl.BlockSpec((None, 1, tn), lambda l, j: (l, 0, j)),
        ],
        out_specs=pl.BlockSpec((None, MOD_ROWS, tn), lambda l, j: (l, 0, j)),
        out_shape=jax.ShapeDtypeStruct((depth, MOD_ROWS, n), F32),
        compiler_params=_cparams(2),
        name="ada_mod",
    )(cvec, w_ada, b_ada.reshape(depth, 1, n))


def _ffn_chunks(d_ff):
    tiles = d_ff // 256
    assert tiles * 256 == d_ff and tiles >= 2
    cut = (tiles + 1) // 2 * 256
    return ((0, cut), (cut, d_ff))


def _ffn_kernel(*refs, n_x_tiles, two_inputs, final, norm_row, mod_base, d_ff):
    if two_inputs:
        x_ref, c_ref, *refs = refs
    else:
        x_ref, *refs = refs
    if final:
        mod_ref, g_ref, w1_ref, w3_ref, w2_ref, fg_ref, o_ref = refs
    else:
        mod_ref, g_ref, w1_ref, w3_ref, w2_ref, o_ref = refs
    shift = mod_ref[mod_base:mod_base + 1, :]
    scale = mod_ref[mod_base + 1:mod_base + 2, :]
    gate = mod_ref[mod_base + 2:mod_base + 3, :]
    for r0 in range(0, x_ref.shape[0], FFN_RB):
        rs = slice(r0, r0 + FFN_RB)
        x = x_ref[rs, :]
        if two_inputs:
            x = jnp.where(pl.program_id(0) < n_x_tiles, x, c_ref[rs, :])
        h = _modnorm(x, g_ref[norm_row:norm_row + 1, :], shift, scale).astype(BF16)
        acc = None
        for lo, hi in _ffn_chunks(d_ff):
            a = _dot(h, w1_ref[:, lo:hi])
            b = _dot(h, w3_ref[:, lo:hi])
            gch = (a * jax.nn.sigmoid(a) * b).astype(BF16)
            part = _dot(gch, w2_ref[lo:hi, :])
            acc = part if acc is None else acc + part
        y = x + (0.5 * gate) * acc
        if final:
            y = y * lax.rsqrt(jnp.mean(y * y, axis=-1, keepdims=True) + EPS) * fg_ref[...]
        o_ref[rs, :] = y


def _mod_row(i, n_x_tiles, tiles_per_batch, n_batch):
    return jnp.where(i < n_x_tiles, i // tiles_per_batch, n_batch)


def _ffn_call(xs, ctx2d, mods, layer, norm_row, mod_base, norm_g, w1, w3, w2, final_g, *,
              n_batch, seq, n_out_rows):
    d = xs.shape[1]
    d_ff = w1.shape[2]
    two_inputs = ctx2d is not None
    tm = TM_TOK if n_out_rows % TM_FFN or seq % TM_FFN or (two_inputs and ctx2d.shape[0] % TM_FFN) else TM_FFN
    n_x_tiles = n_batch * seq // tm
    n_tiles = n_out_rows // tm
    tiles_per_batch = seq // tm
    final = final_g is not None
    in_specs = []
    args = []
    if two_inputs:
        in_specs.append(pl.BlockSpec((tm, d), lambda i: (jnp.minimum(i, n_x_tiles - 1), 0)))
        in_specs.append(pl.BlockSpec((tm, d), lambda i: (jnp.maximum(i - n_x_tiles, 0), 0)))
        args += [xs, ctx2d]
    else:
        in_specs.append(pl.BlockSpec((tm, d), lambda i: (i, 0)))
        args.append(xs)
    in_specs += [
        pl.BlockSpec((None, None, N_MOD, d),
                     lambda i: (layer, _mod_row(i, n_x_tiles, tiles_per_batch, n_batch), 0, 0)),
        _layer_resident(norm_g, layer),
        _layer_resident(w1, layer),
        _layer_resident(w3, layer),
        _layer_resident(w2, layer),
    ]
    args += [mods, norm_g, w1, w3, w2]
    if final:
        in_specs.append(_resident((1, d)))
        args.append(final_g)
    kern = functools.partial(_ffn_kernel, n_x_tiles=n_x_tiles, two_inputs=two_inputs, final=final,
                             norm_row=norm_row, mod_base=mod_base, d_ff=d_ff)
    return pl.pallas_call(
        kern,
        grid=(n_tiles,),
        in_specs=in_specs,
        out_specs=pl.BlockSpec((tm, d), lambda i: (i, 0)),
        out_shape=jax.ShapeDtypeStruct((n_out_rows, d), F32),
        compiler_params=_cparams(1),
        name="ffn",
    )(*args)


def _gelu_tanh(x):
    c = math.sqrt(2.0 / math.pi)
    return x * (0.5 * (1.0 + jnp.tanh(c * (x + 0.044715 * (x * x * x)))))


def _proj_kernel(x_ref, mod_ref, g_ref, w_ref, cos_ref, sin_ref,
                 hx_ref, pa_ref, pb_ref, qT_ref, k_ref, vT_ref, hd_ref):
    lane = lax.broadcasted_iota(jnp.int32, (PROJ_RB, HEAD_W), 1)
    first_half = (lane % (2 * ROPE_NF)) < ROPE_NF
    for r0 in range(0, x_ref.shape[0], PROJ_RB):
        rs = slice(r0, r0 + PROJ_RB)
        h = _modnorm(x_ref[rs, :], g_ref[1:2, :], mod_ref[3:4, :], mod_ref[4:5, :]).astype(BF16)
        hx_ref[rs, :] = h

        def proj(lo, hi):
            return _dot(h, w_ref[:, lo:hi])

        pa = proj(A_OFF, B_OFF)
        pa_ref[rs, 0:CONV_A_W] = pa[:, 0:CONV_A_W].astype(BF16)
        pa_ref[rs, CONV_A_W:] = (pa[:, CONV_A_W:2 * CONV_A_W] * pa[:, 2 * CONV_A_W:]).astype(BF16)

        z = _gelu_tanh(proj(B_OFF, Q_OFF))
        v = z[:, SGU_W:]
        vc = v - jnp.mean(v, axis=-1, keepdims=True)
        vn = vc * lax.rsqrt(jnp.mean(vc * vc, axis=-1, keepdims=True) + EPS)
        pb_ref[rs, 0:SGU_W] = z[:, 0:SGU_W].astype(BF16)
        pb_ref[rs, SGU_W:] = vn.astype(BF16)

        cos = cos_ref[rs, :]
        sin = sin_ref[rs, :]

        def rope(blk):
            partner = jnp.where(first_half, pltpu.roll(blk, HEAD_W - ROPE_NF, 1), pltpu.roll(blk, ROPE_NF, 1))
            return blk * cos + partner * sin

        q = proj(Q_OFF, K_OFF)
        for hh in range(DA_HEADS):
            sl = slice(hh * HEAD_W, (hh + 1) * HEAD_W)
            qr = rope(q[:, sl]) * Q_SCALE
            qT_ref[sl, rs] = qr.T.astype(BF16)
        k = proj(K_OFF, V_OFF)
        for hh in range(DA_HEADS):
            sl = slice(hh * HEAD_W, (hh + 1) * HEAD_W)
            k_ref[rs, sl] = rope(k[:, sl]).astype(BF16)
        vv = proj(V_OFF, D_OFF)
        for hh in range(DA_HEADS):
            sl = slice(hh * HEAD_W, (hh + 1) * HEAD_W)
            vT_ref[sl, rs] = vv[:, sl].T.astype(BF16)

        zd = proj(D_OFF, G_OFF)
        hd_ref[rs, :] = (zd[:, 0:CONF_W] * jax.nn.sigmoid(zd[:, CONF_W:])).astype(BF16)


def _proj_call(xs, mods, layer, norm_g, w_in, cos_t, sin_t, *, n_batch, seq):
    ntok, d = xs.shape
    tm = TM_TOK
    n_x_tiles = n_batch * seq // tm
    tiles_per_batch = seq // tm
    row = lambda i: (i, 0)
    col = lambda i: (0, i)
    rope_row = lambda i: (jnp.where(i < n_x_tiles, i % tiles_per_batch, tiles_per_batch), 0)
    outs = pl.pallas_call(
        _proj_kernel,
        grid=(ntok // tm,),
        in_specs=[
            pl.BlockSpec((tm, d), row),
            pl.BlockSpec((None, None, N_MOD, d),
                         lambda i: (layer, _mod_row(i, n_x_tiles, tiles_per_batch, n_batch), 0, 0)),
            _layer_resident(norm_g, layer),
            pl.BlockSpec((None, d, G_OFF), lambda i: (layer, 0, 0), pipeline_mode=pl.Buffered(1)),
            pl.BlockSpec((tm, HEAD_W), rope_row),
            pl.BlockSpec((tm, HEAD_W), rope_row),
        ],
        out_specs=[
            pl.BlockSpec((tm, d), row),
            pl.BlockSpec((tm, 2 * CONV_A_W), row),
            pl.BlockSpec((tm, 2 * SGU_W), row),
            pl.BlockSpec((DA_W, tm), col),
            pl.BlockSpec((tm, DA_W), row),
            pl.BlockSpec((DA_W, tm), col),
            pl.BlockSpec((tm, CONF_W), row),
        ],
        out_shape=[
            jax.ShapeDtypeStruct((ntok, d), BF16),
            jax.ShapeDtypeStruct((ntok, 2 * CONV_A_W), BF16),
            jax.ShapeDtypeStruct((ntok, 2 * SGU_W), BF16),
            jax.ShapeDtypeStruct((DA_W, ntok), BF16),
            jax.ShapeDtypeStruct((ntok, DA_W), BF16),
            jax.ShapeDtypeStruct((DA_W, ntok), BF16),
            jax.ShapeDtypeStruct((ntok, CONF_W), BF16),
        ],
        compiler_params=_cparams(1),
        name="mixer_proj",
    )(xs, mods, norm_g, w_in, cos_t, sin_t)
    return outs


def _lambda(lam_ref, lam_init):
    lp = lam_ref[...]
    return (jnp.exp(jnp.sum(lp[0:1, :] * lp[1:2, :], axis=-1, keepdims=True))
            - jnp.exp(jnp.sum(lp[2:3, :] * lp[3:4, :], axis=-1, keepdims=True)) + lam_init)


def _finish_head(o0, o1, lam, sg_ref, lam_init):
    oh = o0 - lam * o1
    y = oh * lax.rsqrt(jnp.mean(oh * oh, axis=0, keepdims=True) + EPS)
    return (y * (sg_ref[...] * (1.0 - lam_init))).T.astype(BF16)


def _attn_ctx_kernel(lam_ref, sg_ref, qT_ref, k_ref, vT_ref, o_ref, *, lam_init):
    tq = qT_ref.shape[1]
    lam = _lambda(lam_ref, lam_init)
    sub = lax.broadcasted_iota(jnp.int32, (HEAD_W, tq), 0)
    for hh in range(DA_HEADS):
        hs = slice(hh * HEAD_W, (hh + 1) * HEAD_W)
        qh = qT_ref[hs, :]
        o_maps = []
        for m in range(2):
            in_map = (sub < DA_HEAD_DIM) if m == 0 else (sub >= DA_HEAD_DIM)
            qm = jnp.where(in_map, qh, jnp.zeros_like(qh))
            s = _dot(k_ref[:, hs], qm)
            p = jnp.exp2(s - jnp.max(s, axis=0, keepdims=True))
            o_maps.append(_dot(vT_ref[hs, :], p.astype(BF16)) / jnp.sum(p, axis=0, keepdims=True))
        o_ref[:, hs] = _finish_head(o_maps[0], o_maps[1], lam, sg_ref, lam_init)


def _attn_ctx_call(lam_p, subln_col, layer, qT, k, vT, lam_init, *, n_batch, seq, ctx_len):
    blk0 = n_batch * seq // ctx_len
    kern = functools.partial(_attn_ctx_kernel, lam_init=lam_init)
    return pl.pallas_call(
        kern,
        grid=(n_batch,),
        in_specs=[
            _layer_resident(lam_p, layer),
            _layer_resident(subln_col, layer),
            pl.BlockSpec((DA_W, ctx_len), lambda b: (0, blk0 + b)),
            pl.BlockSpec((ctx_len, DA_W), lambda b: (blk0 + b, 0)),
            pl.BlockSpec((DA_W, ctx_len), lambda b: (0, blk0 + b)),
        ],
        out_specs=pl.BlockSpec((ctx_len, DA_W), lambda b: (b, 0)),
        out_shape=jax.ShapeDtypeStruct((n_batch * ctx_len, DA_W), BF16),
        compiler_params=_cparams(1),
        name="diff_attn_ctx",
    )(lam_p, subln_col, qT, k, vT)


def _attn_pipe_kernel(lam_ref, sg_ref, qT_ref, kl_ref, kc_ref, vl_ref, vc_ref, o_ref, s_scr, mx_scr, *,
                      lam_init, tq):
    step = pl.program_id(0)
    seq = kl_ref.shape[0]
    n_ctx = kc_ref.shape[0]
    blocks = [(kl_ref, vl_ref, r0, KB, r0) for r0 in range(0, seq, KB)]
    blocks.append((kc_ref, vc_ref, 0, n_ctx, seq))

    @pl.when(step == 0)
    def _():
        s_scr[1] = jnp.zeros(s_scr.shape[1:], F32)
        mx_scr[1] = jnp.zeros(mx_scr.shape[1:], F32)

    lam = _lambda(lam_ref, lam_init)

    def body(cur, prv):
        sub = lax.broadcasted_iota(jnp.int32, (HEAD_W, tq), 0)
        qh = qT_ref[...]
        o_maps = []
        for m in range(2):
            in_map = (sub < DA_HEAD_DIM) if m == 0 else (sub >= DA_HEAD_DIM)
            qm = jnp.where(in_map, qh, jnp.zeros_like(qh))
            mx_prev = mx_scr[prv, m, 0:1, :]
            mx_new = None
            lsum = acc = None
            for k_ref, v_ref, r0, rows, srow in blocks:
                s = _dot(k_ref[r0:r0 + rows, :], qm)
                s_scr[cur, m, srow:srow + rows, :] = s
                bm = jnp.max(s, axis=0, keepdims=True)
                mx_new = bm if mx_new is None else jnp.maximum(mx_new, bm)
                p = jnp.exp2(s_scr[prv, m, srow:srow + rows, :] - mx_prev)
                ps = jnp.sum(p, axis=0, keepdims=True)
                pv = _dot(v_ref[:, r0:r0 + rows], p.astype(BF16))
                lsum = ps if lsum is None else lsum + ps
                acc = pv if acc is None else acc + pv
            mx_scr[cur, m] = jnp.broadcast_to(mx_new, (SUBLANES, tq))
            o_maps.append(acc / lsum)
        o_ref[...] = _finish_head(o_maps[0], o_maps[1], lam, sg_ref, lam_init)

    for cur in range(2):
        pl.when(step % 2 == cur)(functools.partial(body, cur, 1 - cur))


def _attn_pipe_call(lam_p, subln_col, layer, qT, k, vT, lam_init, *, n_batch, seq, ctx_len):
    n_x = n_batch * seq
    ctx_blk0 = n_x // ctx_len
    tq = TQ
    n_q = seq // tq
    total = n_batch * DA_HEADS * n_q

    def unit(t):
        return t // (DA_HEADS * n_q), (t // n_q) % DA_HEADS, t % n_q

    def cur(s):
        return unit(jnp.minimum(s, total - 1))

    def prv(s):
        return unit(jnp.maximum(s - 1, 0))

    def q_map(s):
        b, h, j = cur(s)
        return h, b * n_q + j

    def o_map(s):
        b, h, j = prv(s)
        return b * n_q + j, h

    in_specs = [
        _layer_resident(lam_p, layer),
        _layer_resident(subln_col, layer),
        pl.BlockSpec((HEAD_W, tq), q_map),
        pl.BlockSpec((seq, HEAD_W), lambda s: (cur(s)[0], cur(s)[1])),
        pl.BlockSpec((ctx_len, HEAD_W), lambda s: (ctx_blk0 + cur(s)[0], cur(s)[1])),
        pl.BlockSpec((HEAD_W, seq), lambda s: (prv(s)[1], prv(s)[0])),
        pl.BlockSpec((HEAD_W, ctx_len), lambda s: (prv(s)[1], ctx_blk0 + prv(s)[0])),
    ]
    n_keys = seq + ctx_len
    kern = functools.partial(_attn_pipe_kernel, lam_init=lam_init, tq=tq)
    return pl.pallas_call(
        kern,
        grid=(total + 1,),
        in_specs=in_specs,
        out_specs=pl.BlockSpec((tq, HEAD_W), o_map),
        out_shape=jax.ShapeDtypeStruct((n_x, DA_W), BF16),
        scratch_shapes=[pltpu.VMEM((2, 2, n_keys, tq), F32),
                        pltpu.VMEM((2, 2, SUBLANES, tq), F32)],
        compiler_params=_cparams(1),
        name="diff_attn_latent",
    )(lam_p, subln_col, qT, k, k, vT, vT)


def _mix_kernel(x_ref, mod_ref, hx_ref, pa_ref, pap_ref, pan_ref, pb_ref, hd_ref, hdp_ref, hdn_ref, o_ref,
                wg_ref, caw_ref, wa_ref, sw_ref, sb_ref, wb_ref, wc_ref, dw_ref, db_ref, lg_ref, lb_ref,
                wd_ref, wo_ref, out_ref, cxe, hde, hsh, gsc, bin_scr, yd_scr, *, tiles_per_seq, tm):
    d = x_ref.shape[1]
    in_seq = pl.program_id(0) % tiles_per_seq
    prev_zero = in_seq == 0
    next_zero = in_seq == tiles_per_seq - 1
    hx = hx_ref[...]

    def fill(ext, main, prev, nxt):
        ext[HALO:HALO + tm, :] = main.astype(F32)
        ext[0:HALO, :] = jnp.where(prev_zero, 0.0, prev.astype(F32))
        ext[HALO + tm:2 * HALO + tm, :] = jnp.where(next_zero, 0.0, nxt.astype(F32))

    def dwconv(ext, shifted, w_ref, taps, r0, rows):
        acc = None
        for j in range(taps):
            a, r = divmod(HALO + r0 + j - taps // 2, SUBLANES)
            if r == 0 or shifted is None:
                win = ext[pl.ds(a * SUBLANES + r, rows), :]
            else:
                win = shifted[r - 1, pl.ds(a * SUBLANES, rows), :]
            term = w_ref[j:j + 1, :] * win
            acc = term if acc is None else acc + term
        return acc

    def conv_a_task(r):
        def run():
            if r == 0:
                fill(cxe, pa_ref[:, CONV_A_W:], pap_ref[:, CONV_A_W:], pan_ref[:, CONV_A_W:])
            rs = slice(r * CONV_RB, (r + 1) * CONV_RB)
            y = pa_ref[rs, 0:CONV_A_W].astype(F32) * dwconv(cxe, None, caw_ref, CONV_A_K, r * CONV_RB, CONV_RB)
            bin_scr[0, rs, :] = y.astype(BF16)
        return run

    def sgu_task(n):
        def run():
            lane = lax.broadcasted_iota(jnp.int32, (CHUNK, SGU_W), 1)
            rs = slice(n * CHUNK, (n + 1) * CHUNK)
            vn = pb_ref[rs, SGU_W:]
            vbd = jnp.concatenate(
                [jnp.where(lane // SGU_GROUP_W == gi, vn, jnp.zeros_like(vn)) for gi in range(SGU_GROUPS)],
                axis=0)
            s = _dot(sw_ref[...], vbd) + sb_ref[...]
            bin_scr[1, rs, :] = (pb_ref[rs, 0:SGU_W].astype(F32) * s).astype(BF16)
        return run

    def conf_fill_task():
        fill(hde, hd_ref[...], hdp_ref[...], hdn_ref[...])
        n_sh = hsh.shape[1] - SUBLANES
        for r in range(1, SUBLANES):
            hsh[r - 1, 0:n_sh, :] = hde[pl.ds(r, n_sh), :]

    def conf_task(r):
        def run():
            hcv = dwconv(hde, hsh, dw_ref, CONF_K, r * CONV_RB, CONV_RB) + db_ref[...]
            mu = jnp.mean(hcv, axis=-1, keepdims=True)
            hc = hcv - mu
            hn = hc * lax.rsqrt(jnp.mean(hc * hc, axis=-1, keepdims=True) + EPS)
            hn = hn * lg_ref[...] + lb_ref[...]
            bin_scr[2, r * CONV_RB:(r + 1) * CONV_RB, :] = (hn * jax.nn.sigmoid(hn)).astype(BF16)
        return run

    def gate_task(j, half):
        def run():
            cols = slice(half * (d // 2), (half + 1) * (d // 2))
            wcols = slice(j * d + cols.start, j * d + cols.stop)
            gsc[j, :, cols] = jax.nn.sigmoid(_dot(hx, wg_ref[:, wcols]))
        return run

    n_rb = tm // CONV_RB
    n_pieces = 2 * N_BRANCH
    conf_fill_task()
    for r in range(n_rb):
        for p in range(r * n_pieces // n_rb, (r + 1) * n_pieces // n_rb):
            gate_task(p // 2, p % 2)()
        conf_task(r)()
        rs = slice(r * CONV_RB, (r + 1) * CONV_RB)
        yd_scr[rs, :] = _dot(bin_scr[2, rs, :], wd_ref[...])
    for r in range(n_rb):
        conv_a_task(r)()
    merged = gsc[0] * _dot(bin_scr[0], wa_ref[...])
    for n in range(tm // CHUNK):
        sgu_task(n)()
    merged = merged + gsc[1] * _dot(bin_scr[1], wb_ref[...])
    merged = merged + gsc[2] * _dot(o_ref[...], wc_ref[...])
    merged = merged + gsc[3] * yd_scr[...]
    y = _dot(merged.astype(BF16), wo_ref[...])
    out_ref[...] = x_ref[...] + mod_ref[5:6, :] * y


def _mix_call(xs, mods, layer, hx, pa, pb, hd, o, wts, *, n_batch, seq, ctx_len, context):
    ntok, d = xs.shape
    n_x = n_batch * seq
    if context:
        tm = ctx_len
        tiles_per_seq = 1
        n_tiles = n_batch
        blk0 = n_x // tm
        mod_map = lambda i: (layer, n_batch, 0, 0)
    else:
        tm = TM_MIX
        tiles_per_seq = seq // tm
        n_tiles = n_x // tm
        blk0 = 0
        mod_map = lambda i: (layer, i // tiles_per_seq, 0, 0)
    assert tm % CONV_RB == 0 and tm % CHUNK == 0 and seq % tm == 0 and n_x % tm == 0
    hpt = tm // HALO
    n_halo_blocks = ntok // HALO
    row = lambda i: (blk0 + i, 0)
    prev = lambda i: (jnp.maximum((blk0 + i) * hpt - 1, 0), 0)
    nxt = lambda i: (jnp.minimum((blk0 + i + 1) * hpt, n_halo_blocks - 1), 0)
    in_specs = [
        pl.BlockSpec((tm, d), row),
        pl.BlockSpec((None, None, N_MOD, d), mod_map),
        pl.BlockSpec((tm, d), row),
        pl.BlockSpec((tm, 2 * CONV_A_W), row),
        pl.BlockSpec((HALO, 2 * CONV_A_W), prev),
        pl.BlockSpec((HALO, 2 * CONV_A_W), nxt),
        pl.BlockSpec((tm, 2 * SGU_W), row),
        pl.BlockSpec((tm, CONF_W), row),
        pl.BlockSpec((HALO, CONF_W), prev),
        pl.BlockSpec((HALO, CONF_W), nxt),
        pl.BlockSpec((tm, DA_W), lambda i: (i, 0)),
    ]
    in_specs += [_layer_resident(w, layer) for w in wts]
    kern = functools.partial(_mix_kernel, tiles_per_seq=tiles_per_seq, tm=tm)
    ext_rows = tm + 2 * HALO
    return pl.pallas_call(
        kern,
        grid=(n_tiles,),
        in_specs=in_specs,
        out_specs=pl.BlockSpec((tm, d), lambda i: (i, 0)),
        out_shape=jax.ShapeDtypeStruct((n_tiles * tm, d), F32),
        scratch_shapes=[pltpu.VMEM((ext_rows, CONV_A_W), F32),
                        pltpu.VMEM((ext_rows, CONF_W), F32),
                        pltpu.VMEM((SUBLANES - 1, ext_rows, CONF_W), F32),
                        pltpu.VMEM((N_BRANCH, tm, d), F32),
                        pltpu.VMEM((N_BRANCH - 1, tm, CONV_A_W), BF16),
                        pltpu.VMEM((tm, d), F32)],
        compiler_params=_cparams(1),
        name="mixer_merge_ctx" if context else "mixer_merge",
    )(xs, mods, hx, pa, pa, pa, pb, hd, hd, hd, o, *wts)


def _rope_tables(seq, pad_rows):
    n_rows = seq // GRID_W
    row = np.repeat(np.arange(n_rows, dtype=np.float32), GRID_W)
    col = np.tile(np.arange(GRID_W, dtype=np.float32), n_rows)
    inv = (np.float32(ROPE_BASE) ** (-np.arange(ROPE_NF, dtype=np.float32) / np.float32(ROPE_NF))).astype(np.float32)
    ar = (row[:, None] * inv).astype(np.float32)
    ac = (col[:, None] * inv).astype(np.float32)
    cos64 = np.concatenate([np.cos(ar), np.cos(ar), np.cos(ac), np.cos(ac)], axis=1)
    sin64 = np.concatenate([-np.sin(ar), np.sin(ar), -np.sin(ac), np.sin(ac)], axis=1)
    cos_t = np.concatenate([cos64, cos64], axis=1)
    sin_t = np.concatenate([sin64, sin64], axis=1)
    cos_t = np.concatenate([cos_t, np.ones((pad_rows, HEAD_W), np.float32)], axis=0)
    sin_t = np.concatenate([sin_t, np.zeros((pad_rows, HEAD_W), np.float32)], axis=0)
    return jnp.asarray(cos_t, F32), jnp.asarray(sin_t, F32)


def kernel(x, c, ctx, c_ctx, w_ada, b_ada, norm_g, ffn1_w1, ffn1_w3, ffn1_w2, ffn2_w1, ffn2_w3, ffn2_w2, w_in, conv_a_w, w_a_out, sgu_w, sgu_b, w_b_out, lam_p, subln_g, w_c_out, conf_dw, conf_db, conf_ln_g, conf_ln_b, w_d_out, w_o, final_g):
    n_batch, seq, d = x.shape
    ctx_len = ctx.shape[1]
    depth = w_ada.shape[0]
    n_x = n_batch * seq
    n_c = n_batch * ctx_len
    ntok = n_x + n_c
    assert n_batch + 1 <= MOD_ROWS and seq % TM_TOK == 0 and n_c % TM_TOK == 0 and seq % TQ == 0 and seq % TM_MIX == 0
    assert seq % GRID_W == 0 and w_in.shape[2] == G_OFF + N_BRANCH * d
    dims = dict(n_batch=n_batch, seq=seq)

    bf = lambda a: a.astype(BF16)
    f1w1, f1w3, f1w2 = bf(ffn1_w1), bf(ffn1_w3), bf(ffn1_w2)
    f2w1, f2w3, f2w2 = bf(ffn2_w1), bf(ffn2_w3), bf(ffn2_w2)
    w_in_b = bf(w_in[:, :, :G_OFF])
    sw_cat = bf(jnp.transpose(sgu_w, (0, 2, 1, 3)).reshape(depth, CHUNK, SGU_GROUPS * CHUNK))
    sb_full = jnp.repeat(jnp.transpose(sgu_b, (0, 2, 1)), SGU_GROUP_W, axis=2)
    mix_wts = (bf(w_in[:, :, G_OFF:]), conv_a_w, bf(w_a_out), sw_cat, sb_full, bf(w_b_out), bf(w_c_out), conf_dw,
               conf_db.reshape(depth, 1, CONF_W), conf_ln_g.reshape(depth, 1, CONF_W),
               conf_ln_b.reshape(depth, 1, CONF_W), bf(w_d_out), bf(w_o))
    subln_col = subln_g.reshape(depth, HEAD_W, 1)

    cvec = jnp.zeros((MOD_ROWS, d), F32).at[:n_batch].set(c).at[n_batch].set(c_ctx)
    mods = _ada_call(cvec, w_ada, b_ada).reshape(depth, MOD_ROWS, N_MOD, d)
    cos_t, sin_t = _rope_tables(seq, TM_TOK)

    xs = None
    for l in range(depth):
        last = l == depth - 1
        lam_init = 0.8 - 0.6 * float(np.exp(-0.3 * l))
        if l == 0:
            xs = _ffn_call(x.reshape(n_x, d), ctx.reshape(n_c, d), mods, l, 0, 0, norm_g, f1w1, f1w3, f1w2,
                           None, n_out_rows=ntok, **dims)
        else:
            xs = _ffn_call(xs, None, mods, l, 0, 0, norm_g, f1w1, f1w3, f1w2, None, n_out_rows=ntok, **dims)
        hx, pa, pb, qT, k, vT, hd = _proj_call(xs, mods, l, norm_g, w_in_b, cos_t, sin_t, **dims)
        o_x = _attn_pipe_call(lam_p, subln_col, l, qT, k, vT, lam_init, ctx_len=ctx_len, **dims)
        xs_x = _mix_call(xs, mods, l, hx, pa, pb, hd, o_x, mix_wts, ctx_len=ctx_len, context=False, **dims)
        if last:
            xs = _ffn_call(xs_x, None, mods, l, 2, 6, norm_g, f2w1, f2w3, f2w2, final_g.reshape(1, d),
                           n_out_rows=n_x, **dims)
        else:
            o_c = _attn_ctx_call(lam_p, subln_col, l, qT, k, vT, lam_init, ctx_len=ctx_len, **dims)
            xs_c = _mix_call(xs, mods, l, hx, pa, pb, hd, o_c, mix_wts, ctx_len=ctx_len, context=True, **dims)
            xs = _ffn_call(xs_x, xs_c, mods, l, 2, 6, norm_g, f2w1, f2w3, f2w2, None, n_out_rows=ntok, **dims)
    return xs.reshape(n_batch, seq, d)
```
